```python
import jax, jax.numpy as jnp
from jax import lax
import numpy as np

D_MODEL = 2048
BATCH = 4
SEQ = 2048
DEPTH = 1
DEC_BATCH = 32
DEC_SEQ = 1
PAST_LEN = 16384
PAGE_SIZE = 128

N_META = 16
CHUNK = 128
META_PAD = (-N_META) % CHUNK
D_MIX = D_MODEL
D_ATT = D_MIX // 2
ATT_HEAD_DIM = 128
N_ATT_HEADS = D_ATT // ATT_HEAD_DIM
D_MLSTM = D_MIX - D_ATT
N_MLSTM_HEADS = 4
MLSTM_V_DIM = D_MLSTM // N_MLSTM_HEADS
MLSTM_QK_DIM = MLSTM_V_DIM // 2
D_MLSTM_QK = N_MLSTM_HEADS * MLSTM_QK_DIM
N_EXPERTS = 32
TOP_K = 4
D_FF = D_MODEL
SWIGLU_ALPHA = 1.702
SWIGLU_LIMIT = 7.0
MOE_BLOCK = 128
DEEPNORM_ALPHA = (2.0 * DEPTH) ** 0.25
DEEPNORM_BETA = (8.0 * DEPTH) ** -0.25
LN_EPS = 1e-5
RMS_EPS = 1e-6
NEG = -1e30
IN_SIZES = (D_ATT, D_ATT, D_ATT, N_ATT_HEADS, D_MLSTM_QK, D_MLSTM_QK, D_MLSTM, N_MLSTM_HEADS, N_MLSTM_HEADS, D_MLSTM)
IN_WIDTH = sum(IN_SIZES)
SPLIT_POINTS = tuple(int(s) for s in np.cumsum(IN_SIZES)[:-1])

kernel_name = 'hymba_fox_mlstm_moe_step'


def layer_norm(x, g, b):
    xf = x.astype(jnp.float32)
    mu = xf.mean(-1, keepdims=True)
    var = jnp.square(xf - mu).mean(-1, keepdims=True)
    return ((xf - mu) * lax.rsqrt(var + LN_EPS) * g + b).astype(x.dtype)


def rms_norm(x, g):
    xf = x.astype(jnp.float32)
    return (xf * lax.rsqrt(jnp.mean(xf * xf, -1, keepdims=True) + RMS_EPS) * g).astype(x.dtype)


def pad_front(t, value):
    return jnp.pad(t, [(0, 0), (META_PAD, 0)] + [(0, 0)] * (t.ndim - 2), constant_values=value)


def project_mixers(h, w_in, b_att_f, g_att_q, g_att_k, b_m_i, b_m_f):
    B, T, _ = h.shape
    qa, ka, va, fa, qm, km, vm, im, fm, om = jnp.split(h @ w_in, SPLIT_POINTS, axis=-1)
    qa = rms_norm(qa.reshape(B, T, N_ATT_HEADS, ATT_HEAD_DIM), g_att_q)
    ka = rms_norm(ka.reshape(B, T, N_ATT_HEADS, ATT_HEAD_DIM), g_att_k)
    va = va.reshape(B, T, N_ATT_HEADS, ATT_HEAD_DIM)
    lfa = jax.nn.log_sigmoid(fa.astype(jnp.float32) + b_att_f)
    qm = qm.reshape(B, T, N_MLSTM_HEADS, MLSTM_QK_DIM)
    km = km.reshape(B, T, N_MLSTM_HEADS, MLSTM_QK_DIM) * (MLSTM_QK_DIM ** -0.5)
    vm = vm.reshape(B, T, N_MLSTM_HEADS, MLSTM_V_DIM)
    igm = im.astype(jnp.float32) + b_m_i
    lfm = jax.nn.log_sigmoid(fm.astype(jnp.float32) + b_m_f)
    og = jax.nn.sigmoid(om)
    return (qa, ka, va, lfa), (qm, km, vm, igm, lfm), og


def fox_prompt(q, k, v, lf):
    B, L = q.shape[:2]
    c = jnp.cumsum(lf, axis=1).transpose(0, 2, 1)
    kpos = jnp.arange(L)
    scale = ATT_HEAD_DIM ** -0.5

    def block(i):
        start = i * CHUNK
        qb = lax.dynamic_slice_in_dim(q, start, CHUNK, axis=1)
        cb = lax.dynamic_slice_in_dim(c, start, CHUNK, axis=2)
        qpos = start + jnp.arange(CHUNK)
        s = jnp.einsum('bqhd,bkhd->bhqk', qb, k).astype(jnp.float32) * scale + cb[..., :, None] - c[..., None, :]
        mask = (kpos[None, :] <= qpos[:, None]) & (kpos[None, :] >= META_PAD)
        p = jax.nn.softmax(jnp.where(mask, s, NEG), axis=-1)
        return jnp.einsum('bhqk,bkhd->bqhd', p.astype(v.dtype), v)

    out = lax.map(block, jnp.arange(L // CHUNK))
    return jnp.moveaxis(out, 0, 1).reshape(B, L, N_ATT_HEADS, ATT_HEAD_DIM)


def fox_sample(q, k, v, lf, cache_k, cache_v, cache_logf, page_table, layer):
    DB, T = q.shape[:2]
    k_past = cache_k[layer, page_table].reshape(DB, -1, N_ATT_HEADS, ATT_HEAD_DIM)
    v_past = cache_v[layer, page_table].reshape(DB, -1, N_ATT_HEADS, ATT_HEAD_DIM)
    lf_past = cache_logf[layer, page_table].reshape(DB, -1, N_ATT_HEADS).astype(jnp.float32)
    n_past = k_past.shape[1]
    c_past = jnp.cumsum(lf_past, axis=1)
    c_new = c_past[:, -1:] + jnp.cumsum(lf, axis=1)
    c_past, c_new = c_past.transpose(0, 2, 1), c_new.transpose(0, 2, 1)
    scale = ATT_HEAD_DIM ** -0.5
    s_past = jnp.einsum('bthd,bshd->bhts', q, k_past).astype(jnp.float32) * scale + c_new[..., :, None] - c_past[..., None, :]
    s_new = jnp.einsum('bthd,bshd->bhts', q, k).astype(jnp.float32) * scale + c_new[..., :, None] - c_new[..., None, :]
    s_new = jnp.where(jnp.tril(jnp.ones((T, T), bool)), s_new, NEG)
    p = jax.nn.softmax(jnp.concatenate([s_past, s_new], axis=-1), axis=-1).astype(v.dtype)
    return (jnp.einsum('bhts,bshd->bthd', p[..., :n_past], v_past)
            + jnp.einsum('bhts,bshd->bthd', p[..., n_past:], v))


def mlstm_chunk(state, xs):
    C0, n0, m0 = state
    q, k, v, ig, lf = xs
    q, k, v = q.astype(jnp.float32), k.astype(jnp.float32), v.astype(jnp.float32)
    T = q.shape[1]
    ig = ig.transpose(0, 2, 1)
    b = jnp.cumsum(lf, axis=1).transpose(0, 2, 1)
    causal = jnp.tril(jnp.ones((T, T), bool))
    dmat = jnp.where(causal, b[..., :, None] - b[..., None, :] + ig[..., None, :], NEG)
    inter = b + m0[..., None]
    m = jnp.maximum(inter, dmat.max(-1))
    w_intra = jnp.exp(dmat - m[..., None])
    w_inter = jnp.exp(inter - m)
    s = jnp.einsum('bthk,bshk->bhts', q, k) * w_intra
    num = jnp.einsum('bhts,bshv->bhtv', s, v) + w_inter[..., None] * jnp.einsum('bthk,bhkv->bhtv', q, C0)
    den = s.sum(-1) + w_inter * jnp.einsum('bthk,bhk->bht', q, n0)
    h = num / jnp.maximum(jnp.abs(den), jnp.exp(-m))[..., None]
    m_new = m[..., -1]
    w_state = jnp.exp(b[..., -1:] - b + ig - m_new[..., None])
    decay = jnp.exp(b[..., -1] + m0 - m_new)
    C_new = decay[..., None, None] * C0 + jnp.einsum('bhs,bshk,bshv->bhkv', w_state, k, v)
    n_new = decay[..., None] * n0 + jnp.einsum('bhs,bshk->bhk', w_state, k)
    return (C_new, n_new, m_new), h.transpose(0, 2, 1, 3)


def mlstm_prompt(q, k, v, ig, lf):
    batch, length = q.shape[:2]
    n_chunks = length // CHUNK

    def chunks(t):
        return jnp.moveaxis(t.reshape(batch, n_chunks, CHUNK, *t.shape[2:]), 1, 0)

    init = (jnp.zeros((batch, N_MLSTM_HEADS, MLSTM_QK_DIM, MLSTM_V_DIM), jnp.float32),
            jnp.zeros((batch, N_MLSTM_HEADS, MLSTM_QK_DIM), jnp.float32),
            jnp.zeros((batch, N_MLSTM_HEADS), jnp.float32))
    state, h = lax.scan(mlstm_chunk, init, (chunks(q), chunks(k), chunks(v), chunks(ig), chunks(lf)))
    return state, jnp.moveaxis(h, 0, 1).reshape(batch, length, N_MLSTM_HEADS, MLSTM_V_DIM)


def moe_ffn(x, w_router, b_router, w_gate_up, b_gate_up, w_down, b_down):
    lead = x.shape[:-1]
    xt = x.reshape(-1, D_MODEL)
    n_tok = xt.shape[0]
    n_pairs = n_tok * TOP_K
    logits = (xt @ w_router).astype(jnp.float32) + b_router
    top_logit, top_e = lax.top_k(logits, TOP_K)
    gate = jax.nn.softmax(top_logit, axis=-1).reshape(-1)
    pair_e = top_e.reshape(-1)
    order = jnp.argsort(pair_e).astype(jnp.int32)
    e_sorted = pair_e[order]
    counts = jnp.bincount(pair_e, length=N_EXPERTS)
    padded = (counts + MOE_BLOCK - 1) // MOE_BLOCK * MOE_BLOCK
    pad_end = jnp.cumsum(padded)
    pad_start = pad_end - padded
    start = jnp.cumsum(counts) - counts
    slot = pad_start[e_sorted] + jnp.arange(n_pairs) - start[e_sorted]
    n_blocks = -(-(n_pairs + N_EXPERTS * (MOE_BLOCK - 1)) // MOE_BLOCK)
    slot_pair = jnp.full((n_blocks * MOE_BLOCK,), n_pairs, jnp.int32).at[slot].set(order)
    slot_tok = slot_pair // TOP_K
    slot_gate = jnp.concatenate([gate, jnp.zeros((1,), jnp.float32)])[slot_pair]
    block_e = jnp.minimum(jnp.searchsorted(pad_end, jnp.arange(n_blocks) * MOE_BLOCK, side='right'), N_EXPERTS - 1)
    x_pad = jnp.concatenate([xt, jnp.zeros((1, D_MODEL), xt.dtype)], axis=0)

    def expert_block(args):
        tok, e = args
        gu = x_pad[tok] @ w_gate_up[e] + b_gate_up[e]
        g = jnp.minimum(gu[:, :D_FF], SWIGLU_LIMIT)
        u = jnp.clip(gu[:, D_FF:], -SWIGLU_LIMIT, SWIGLU_LIMIT)
        act = g * jax.nn.sigmoid(SWIGLU_ALPHA * g) * (u + 1.0)
        return act @ w_down[e] + b_down[e]

    y = lax.map(expert_block, (slot_tok.reshape(n_blocks, MOE_BLOCK), block_e))
    y = y.reshape(-1, D_MODEL) * slot_gate[:, None].astype(y.dtype)
    y = jax.ops.segment_sum(y, slot_tok, num_segments=n_tok + 1)[:n_tok]
    return y.reshape(*lead, D_MODEL)


def residual_block(h, att, hm, og, g_m_out, w_out, ln1_g, ln1_b, w_router, b_router,
                   w_gate_up, b_gate_up, w_down, b_down, ln2_g, ln2_b):
    B, T = h.shape[:2]
    hm = rms_norm(hm, g_m_out.reshape(N_MLSTM_HEADS, MLSTM_V_DIM)).astype(h.dtype).reshape(B, T, D_MLSTM)
    mix = jnp.concatenate([att.reshape(B, T, D_ATT).astype(h.dtype), og * hm], axis=-1) @ w_out
    h = layer_norm(DEEPNORM_ALPHA * h + mix, ln1_g, ln1_b)
    ffn = moe_ffn(h, w_router, b_router, w_gate_up, b_gate_up, w_down, b_down)
    return layer_norm(DEEPNORM_ALPHA * h + ffn, ln2_g, ln2_b)


def setup_inputs(seed: int = 0) -> dict:
    key = jax.random.key(seed)
    ks = jax.random.split(key, 32)

    def nrm(i, shape, scale=1.0):
        return scale * jax.random.normal(ks[i], shape, jnp.float32)

    def uni(i, shape, lo, hi):
        return jax.random.uniform(ks[i], shape, jnp.float32, lo, hi)

    n_pages = PAST_LEN // PAGE_SIZE
    n_used = DEC_BATCH * n_pages
    n_pool = n_used + n_used // 4
    page_table = jax.random.permutation(ks[0], n_pool)[:n_used].reshape(DEC_BATCH, n_pages).astype(jnp.int32)
    return {
        'x_prompt': nrm(1, (BATCH, SEQ, D_MODEL)),
        'x_sample': nrm(2, (DEC_BATCH, DEC_SEQ, D_MODEL)),
        'cache_k': nrm(3, (DEPTH, n_pool, PAGE_SIZE, N_ATT_HEADS, ATT_HEAD_DIM)),
        'cache_v': nrm(4, (DEPTH, n_pool, PAGE_SIZE, N_ATT_HEADS, ATT_HEAD_DIM)),
        'cache_logf': jax.nn.log_sigmoid(3.5 + nrm(5, (DEPTH, n_pool, PAGE_SIZE, N_ATT_HEADS))),
        'state_C': nrm(6, (DEPTH, DEC_BATCH, N_MLSTM_HEADS, MLSTM_QK_DIM, MLSTM_V_DIM), 0.3),
        'state_n': nrm(7, (DEPTH, DEC_BATCH, N_MLSTM_HEADS, MLSTM_QK_DIM), 0.3),
        'state_m': nrm(8, (DEPTH, DEC_BATCH, N_MLSTM_HEADS)) - 1.0,
        'page_table': page_table,
        'meta_tokens': nrm(9, (N_META, D_MODEL)),
        'ln_in_g': 1.0 + nrm(10, (D_MODEL,), 0.02),
        'ln_in_b': nrm(11, (D_MODEL,), 0.02),
        'w_in': nrm(12, (DEPTH, D_MODEL, IN_WIDTH), D_MODEL ** -0.5),
        'b_att_f': uni(13, (DEPTH, N_ATT_HEADS), 2.0, 5.0),
        'g_att_q': 1.0 + nrm(14, (DEPTH, ATT_HEAD_DIM), 0.02),
        'g_att_k': 1.0 + nrm(15, (DEPTH, ATT_HEAD_DIM), 0.02),
        'b_m_i': nrm(16, (DEPTH, N_MLSTM_HEADS), 0.5) - 1.0,
        'b_m_f': uni(17, (DEPTH, N_MLSTM_HEADS), 3.0, 6.0),
        'g_m_out': 1.0 + nrm(18, (DEPTH, D_MLSTM), 0.02),
        'w_out': nrm(19, (DEPTH, D_MIX, D_MODEL), DEEPNORM_BETA * D_MIX ** -0.5),
        'ln1_g': 1.0 + nrm(20, (DEPTH, D_MODEL), 0.02),
        'ln1_b': nrm(21, (DEPTH, D_MODEL), 0.02),
        'w_router': nrm(22, (DEPTH, D_MODEL, N_EXPERTS), D_MODEL ** -0.5),
        'b_router': nrm(23, (DEPTH, N_EXPERTS), 0.01),
        'w_gate_up': nrm(24, (DEPTH, N_EXPERTS, D_MODEL, 2 * D_FF), D_MODEL ** -0.5),
        'b_gate_up': nrm(25, (DEPTH, N_EXPERTS, 2 * D_FF), 0.01),
        'w_down': nrm(26, (DEPTH, N_EXPERTS, D_FF, D_MODEL), DEEPNORM_BETA * D_FF ** -0.5),
        'b_down': nrm(27, (DEPTH, N_EXPERTS, D_MODEL), 0.01),
        'ln2_g': 1.0 + nrm(28, (DEPTH, D_MODEL), 0.02),
        'ln2_b': nrm(29, (DEPTH, D_MODEL), 0.02),
    }


def reference(x_prompt, x_sample, cache_k, cache_v, cache_logf, state_C, state_n, state_m, page_table,
              meta_tokens, ln_in_g, ln_in_b, w_in, b_att_f, g_att_q, g_att_k, b_m_i, b_m_f, g_m_out, w_out,
              ln1_g, ln1_b, w_router, b_router, w_gate_up, b_gate_up, w_down, b_down, ln2_g, ln2_b):
    batch = x_prompt.shape[0]
    meta = jnp.broadcast_to(meta_tokens.astype(x_prompt.dtype)[None], (batch, N_META, D_MODEL))
    hp = layer_norm(jnp.concatenate([meta, x_prompt], axis=1), ln_in_g, ln_in_b)
    hs = layer_norm(x_sample, ln_in_g, ln_in_b)
    kp_l, vp_l, lfp_l, cp_l, np_l, mp_l = [], [], [], [], [], []
    ks_l, vs_l, lfs_l, cs_l, ns_l, ms_l = [], [], [], [], [], []
    for l in range(DEPTH):
        proj_w = (w_in[l], b_att_f[l], g_att_q[l], g_att_k[l], b_m_i[l], b_m_f[l])
        block_w = (g_m_out[l], w_out[l], ln1_g[l], ln1_b[l], w_router[l], b_router[l],
                   w_gate_up[l], b_gate_up[l], w_down[l], b_down[l], ln2_g[l], ln2_b[l])
        (qa, ka, va, lfa), (qm, km, vm, igm, lfm), og = project_mixers(hp, *proj_w)
        att = fox_prompt(pad_front(qa, 0.0), pad_front(ka, 0.0), pad_front(va, 0.0), pad_front(lfa, 0.0))[:, META_PAD:]
        (c_p, n_p, m_p), hm = mlstm_prompt(pad_front(qm, 0.0), pad_front(km, 0.0), pad_front(vm, 0.0),
                                           pad_front(igm, NEG), pad_front(lfm, 0.0))
        kp_l.append(ka); vp_l.append(va); lfp_l.append(lfa)
        cp_l.append(c_p); np_l.append(n_p); mp_l.append(m_p)
        hp = residual_block(hp, att, hm[:, META_PAD:], og, *block_w)
        (qa, ka, va, lfa), (qm, km, vm, igm, lfm), og = project_mixers(hs, *proj_w)
        att = fox_sample(qa, ka, va, lfa, cache_k, cache_v, cache_logf, page_table, l)
        init = (state_C[l].astype(jnp.float32), state_n[l].astype(jnp.float32), state_m[l].astype(jnp.float32))
        (c_s, n_s, m_s), hm = mlstm_chunk(init, (qm, km, vm, igm, lfm))
        ks_l.append(ka); vs_l.append(va); lfs_l.append(lfa)
        cs_l.append(c_s); ns_l.append(n_s); ms_l.append(m_s)
        hs = residual_block(hs, att, hm, og, *block_w)
    return (hp[:, N_META:], hs,
            jnp.stack(kp_l), jnp.stack(vp_l), jnp.stack(lfp_l), jnp.stack(cp_l), jnp.stack(np_l), jnp.stack(mp_l),
            jnp.stack(ks_l), jnp.stack(vs_l), jnp.stack(lfs_l), jnp.stack(cs_l), jnp.stack(ns_l), jnp.stack(ms_l))
```

```python
import functools

import jax
import jax.numpy as jnp
from jax import lax
from jax.experimental import pallas as pl
from jax.experimental.pallas import tpu as pltpu

F32 = jnp.float32
BF16 = jnp.bfloat16
I32 = jnp.int32

N_META = 16
CHUNK = 128
ATT_HEAD_DIM = 128
N_ATT_HEADS = 8
N_MLSTM_HEADS = 4
MLSTM_QK_DIM = 128
MLSTM_V_DIM = 256
N_EXPERTS = 32
TOP_K = 4
SWIGLU_ALPHA = 1.702
SWIGLU_LIMIT = 7.0
LN_EPS = 1e-5
RMS_EPS = 1e-6
NEG = -1e30

LANES = 128
GATE_LF_ATT = 0
GATE_IG = N_ATT_HEADS
GATE_LF_M = GATE_IG + N_MLSTM_HEADS
N_GATES = GATE_LF_M + N_MLSTM_HEADS

VMEM_LIMIT = 56 * 1024 * 1024

MOE_ROWS = 512
MOE_SUB = 128
MOE_FT = 512
PAGES_PER_STEP = 8


def _cparams(*sem):
    return pltpu.CompilerParams(dimension_semantics=sem, vmem_limit_bytes=VMEM_LIMIT)


def _split2(x):
    hi = x.astype(BF16)
    lo = (x - hi.astype(F32)).astype(BF16)
    return hi, lo


def _split3(x):
    hi = x.astype(BF16)
    r1 = x - hi.astype(F32)
    mid = r1.astype(BF16)
    lo = (r1 - mid.astype(F32)).astype(BF16)
    return hi, mid, lo


def _dot(a, b):
    return jnp.dot(a, b, preferred_element_type=F32)


def _dot_nt(a, b):
    return lax.dot_general(a, b, (((1,), (1,)), ((), ())), preferred_element_type=F32)


def _dot_tn(a, b):
    return lax.dot_general(a, b, (((0,), (0,)), ((), ())), preferred_element_type=F32)


def _dot3(x_f32, w_hi, w_lo):
    x_hi, x_lo = _split2(x_f32)
    return _dot(x_hi, w_hi) + (_dot(x_lo, w_hi) + _dot(x_hi, w_lo))


def _layer_norm(x, g, b):
    mu = jnp.mean(x, axis=-1, keepdims=True)
    xc = x - mu
    var = jnp.mean(xc * xc, axis=-1, keepdims=True)
    return xc * lax.rsqrt(var + LN_EPS) * g + b


def _log_sigmoid(z):
    return jnp.minimum(z, 0.0) - jnp.log(1.0 + jnp.exp(-jnp.abs(z)))


def _sigmoid(z):
    return 1.0 / (1.0 + jnp.exp(-z))


def _proj_kernel(x_ref, lng_ref, lnb_ref, w_ref, wgh_ref, wgl_ref, gb_ref, gq_ref, gk_ref,
                 h_ref, q_ref, k32_ref, kb_ref, v32_ref, vb_ref, g_ref, qm_ref, km_ref, vm_ref, og_ref,
                 *, tm, valid_len):
    d_att = N_ATT_HEADS * ATT_HEAD_DIM
    d_mqk = N_MLSTM_HEADS * MLSTM_QK_DIM
    d_mv = N_MLSTM_HEADS * MLSTM_V_DIM
    h = _layer_norm(x_ref[0], lng_ref[...], lnb_ref[...])
    h_ref[0] = h
    hb = h.astype(BF16)

    def seg(c0, width):
        return _dot(hb, w_ref[:, c0:c0 + width])

    scale = ATT_HEAD_DIM ** -0.5
    qa = seg(0, d_att)
    ka = seg(d_att, d_att)
    for hh in range(N_ATT_HEADS):
        sl = slice(hh * ATT_HEAD_DIM, (hh + 1) * ATT_HEAD_DIM)
        qh = qa[:, sl]
        qn = qh * lax.rsqrt(jnp.mean(qh * qh, axis=-1, keepdims=True) + RMS_EPS) * gq_ref[...]
        q_ref[0, :, sl] = (qn * scale).astype(BF16)
        kh = ka[:, sl]
        kn = kh * lax.rsqrt(jnp.mean(kh * kh, axis=-1, keepdims=True) + RMS_EPS) * gk_ref[...]
        k32_ref[0, :, sl] = kn
        kb_ref[0, :, sl] = kn.astype(BF16)
    va = seg(2 * d_att, d_att)
    v32_ref[0] = va
    vb_ref[0] = va.astype(BF16)
    c0 = 3 * d_att
    qm_ref[0] = seg(c0, d_mqk).astype(BF16)
    km_ref[0] = (seg(c0 + d_mqk, d_mqk) * (MLSTM_QK_DIM ** -0.5)).astype(BF16)
    vm_ref[0] = seg(c0 + 2 * d_mqk, d_mv).astype(BF16)
    og_ref[0] = _sigmoid(seg(c0 + 2 * d_mqk + d_mv, d_mv))

    z = _dot3(h, wgh_ref[...], wgl_ref[...]) + gb_ref[...]
    lane = lax.broadcasted_iota(I32, z.shape, 1)
    row = pl.program_id(1) * tm + lax.broadcasted_iota(I32, z.shape, 0)
    is_ig = jnp.logical_and(lane >= GATE_IG, lane < GATE_LF_M)
    gates = jnp.where(is_ig, z, jnp.where(lane < N_GATES, _log_sigmoid(z), 0.0))
    gates = jnp.where(row < valid_len, gates, jnp.where(is_ig, NEG, 0.0))
    g_ref[0] = gates


def _proj(x, valid_len, tm, prm):
    nb, lp, d = x.shape
    d_att = N_ATT_HEADS * ATT_HEAD_DIM
    d_mqk = N_MLSTM_HEADS * MLSTM_QK_DIM
    d_mv = N_MLSTM_HEADS * MLSTM_V_DIM
    n_main = prm["w_main"].shape[1]
    grid = (nb, lp // tm)
    row = lambda w: pl.BlockSpec((1, tm, w), lambda b, i: (b, i, 0))
    const = lambda r, c: pl.BlockSpec((r, c), lambda b, i: (0, 0), pipeline_mode=pl.Buffered(1))
    out_shape = (
        jax.ShapeDtypeStruct((nb, lp, d), F32),
        jax.ShapeDtypeStruct((nb, lp, d_att), BF16),
        jax.ShapeDtypeStruct((nb, valid_len, d_att), F32),
        jax.ShapeDtypeStruct((nb, lp, d_att), BF16),
        jax.ShapeDtypeStruct((nb, valid_len, d_att), F32),
        jax.ShapeDtypeStruct((nb, lp, d_att), BF16),
        jax.ShapeDtypeStruct((nb, lp, LANES), F32),
        jax.ShapeDtypeStruct((nb, lp, d_mqk), BF16),
        jax.ShapeDtypeStruct((nb, lp, d_mqk), BF16),
        jax.ShapeDtypeStruct((nb, lp, d_mv), BF16),
        jax.ShapeDtypeStruct((nb, lp, d_mv), F32),
    )
    out_specs = (row(d), row(d_att), row(d_att), row(d_att), row(d_att), row(d_att), row(LANES),
                 row(d_mqk), row(d_mqk), row(d_mv), row(d_mv))
    return pl.pallas_call(
        functools.partial(_proj_kernel, tm=tm, valid_len=valid_len),
        grid=grid,
        in_specs=[row(d), const(1, d), const(1, d), const(d, n_main), const(d, LANES), const(d, LANES),
                  const(1, LANES), const(1, ATT_HEAD_DIM), const(1, ATT_HEAD_DIM)],
        out_specs=out_specs,
        out_shape=out_shape,
        compiler_params=_cparams("parallel", "arbitrary"),
        name="proj",
    )(x, prm["ln_in_g"], prm["ln_in_b"], prm["w_main"], prm["wg_hi"], prm["wg_lo"], prm["gate_bias"],
      prm["g_att_q"], prm["g_att_k"])


def _gate_scan_kernel(g_ref, cg_ref, cl_ref, cgt_ref, clt_ref, gt_ref, *, n_chunks):
    r = lax.broadcasted_iota(I32, (CHUNK, CHUNK), 0)
    c = lax.broadcasted_iota(I32, (CHUNK, CHUNK), 1)
    tri = jnp.where(c <= r, 1.0, 0.0).astype(BF16)
    carry = jnp.zeros((1, LANES), F32)
    for ci in range(n_chunks):
        sl = slice(ci * CHUNK, (ci + 1) * CHUNK)
        g = g_ref[0, sl, :]
        hi, mid, lo = _split3(g)
        loc = _dot(tri, hi) + (_dot(tri, mid) + _dot(tri, lo))
        glob = loc + carry
        carry = glob[CHUNK - 1:CHUNK, :]
        cg_ref[0, sl, :] = glob
        cl_ref[0, sl, :] = loc
        cgt_ref[0, ci] = glob.T
        clt_ref[0, ci] = loc.T
        gt_ref[0, ci] = g.T


def _gate_scan(gates):
    nb, lp, _ = gates.shape
    nc = lp // CHUNK
    col = pl.BlockSpec((1, lp, LANES), lambda b: (b, 0, 0))
    rowt = pl.BlockSpec((1, nc, LANES, CHUNK), lambda b: (b, 0, 0, 0))
    return pl.pallas_call(
        functools.partial(_gate_scan_kernel, n_chunks=nc),
        grid=(nb,),
        in_specs=[col],
        out_specs=(col, col, rowt, rowt, rowt),
        out_shape=(jax.ShapeDtypeStruct((nb, lp, LANES), F32), jax.ShapeDtypeStruct((nb, lp, LANES), F32),
                   jax.ShapeDtypeStruct((nb, nc, LANES, CHUNK), F32),
                   jax.ShapeDtypeStruct((nb, nc, LANES, CHUNK), F32),
                   jax.ShapeDtypeStruct((nb, nc, LANES, CHUNK), F32)),
        compiler_params=_cparams("parallel"),
        name="gate_scan",
    )(gates)


def _fox_kernel(q_ref, k_ref, v_ref, cg_ref, cgt_ref, o_ref, m_sc, l_sc, acc_sc):
    i = pl.program_id(1)
    m_sc[...] = jnp.full(m_sc.shape, NEG, F32)
    l_sc[...] = jnp.zeros(l_sc.shape, F32)
    acc_sc[...] = jnp.zeros(acc_sc.shape, F32)
    r = lax.broadcasted_iota(I32, (CHUNK, CHUNK), 0)
    c = lax.broadcasted_iota(I32, (CHUNK, CHUNK), 1)
    causal = c <= r

    def step(j, masked):
        rows = pl.ds(pl.multiple_of(j * CHUNK, CHUNK), CHUNK)
        for hh in range(N_ATT_HEADS):
            sl = slice(hh * ATT_HEAD_DIM, (hh + 1) * ATT_HEAD_DIM)
            s = _dot_nt(q_ref[0, :, sl], k_ref[0, rows, sl])
            s = s + (cg_ref[0, :, hh:hh + 1] - cgt_ref[0, j, hh:hh + 1, :])
            if masked:
                s = jnp.where(causal, s, NEG)
            m_old = m_sc[hh]
            m_new = jnp.maximum(m_old, jnp.max(s, axis=-1, keepdims=True))
            p = jnp.exp(s - m_new)
            alpha = jnp.exp(m_old - m_new)
            l_sc[hh] = alpha * l_sc[hh] + jnp.sum(p, axis=-1, keepdims=True)
            acc_sc[hh] = alpha * acc_sc[hh] + _dot(p.astype(BF16), v_ref[0, rows, sl])
            m_sc[hh] = m_new

    def body(j, carry):
        step(j, False)
        return carry

    lax.fori_loop(0, i, body, 0)
    step(i, True)
    for hh in range(N_ATT_HEADS):
        sl = slice(hh * ATT_HEAD_DIM, (hh + 1) * ATT_HEAD_DIM)
        o_ref[0, :, sl] = (acc_sc[hh] / l_sc[hh]).astype(o_ref.dtype)


def _fox_prompt(q, kb, vb, cum_g, cum_gt):
    nb, lp, d_att = q.shape
    nc = lp // CHUNK
    blk = pl.BlockSpec((1, CHUNK, d_att), lambda b, i: (b, i, 0))
    full = pl.BlockSpec((1, lp, d_att), lambda b, i: (b, 0, 0))
    return pl.pallas_call(
        _fox_kernel,
        grid=(nb, nc),
        in_specs=[blk, full, full,
                  pl.BlockSpec((1, CHUNK, LANES), lambda b, i: (b, i, 0)),
                  pl.BlockSpec((1, nc, LANES, CHUNK), lambda b, i: (b, 0, 0, 0))],
        out_specs=blk,
        out_shape=jax.ShapeDtypeStruct((nb, lp, d_att), BF16),
        scratch_shapes=[pltpu.VMEM((N_ATT_HEADS, CHUNK, 1), F32), pltpu.VMEM((N_ATT_HEADS, CHUNK, 1), F32),
                        pltpu.VMEM((N_ATT_HEADS, CHUNK, ATT_HEAD_DIM), F32)],
        compiler_params=_cparams("parallel", "arbitrary"),
        name="fox_prompt",
    )(q, kb, vb, cum_g, cum_gt)


def _mlstm_kernel(q_ref, k_ref, v_ref, g_ref, cl_ref, gt_ref, clt_ref,
                  h_ref, c_out_ref, n_out_ref, m_out_ref, c_sc, n_sc, m_sc):
    ci = pl.program_id(1)

    @pl.when(ci == 0)
    def _():
        c_sc[...] = jnp.zeros(c_sc.shape, F32)
        n_sc[...] = jnp.zeros(n_sc.shape, F32)
        m_sc[...] = jnp.zeros(m_sc.shape, F32)

    r = lax.broadcasted_iota(I32, (CHUNK, CHUNK), 0)
    c = lax.broadcasted_iota(I32, (CHUNK, CHUNK), 1)
    causal = c <= r
    last = slice(CHUNK - 1, CHUNK)
    for hh in range(N_MLSTM_HEADS):
        qk = slice(hh * MLSTM_QK_DIM, (hh + 1) * MLSTM_QK_DIM)
        vv = slice(hh * MLSTM_V_DIM, (hh + 1) * MLSTM_V_DIM)
        ig_l, lf_l = GATE_IG + hh, GATE_LF_M + hh
        b_col = cl_ref[0, :, lf_l:lf_l + 1]
        ig_col = g_ref[0, :, ig_l:ig_l + 1]
        b_row = clt_ref[0, 0, lf_l:lf_l + 1, :]
        ig_row = gt_ref[0, 0, ig_l:ig_l + 1, :]
        m0 = m_sc[hh][:, 0:1]
        c0 = c_sc[hh]
        n0 = n_sc[hh]
        qh = q_ref[0, :, qk]
        kh = k_ref[0, :, qk]
        vh = v_ref[0, :, vv]

        dmat = jnp.where(causal, b_col - b_row + ig_row, NEG)
        inter = b_col + m0
        mt = jnp.maximum(inter, jnp.max(dmat, axis=-1, keepdims=True))
        w_intra = jnp.exp(dmat - mt)
        w_inter = jnp.exp(inter - mt)
        s = _dot_nt(qh, kh) * w_intra
        num = _dot(s.astype(BF16), vh) + w_inter * _dot(qh, c0.astype(BF16))
        den = jnp.sum(s, axis=-1, keepdims=True) + w_inter * jnp.sum(qh.astype(F32) * n0, axis=-1, keepdims=True)
        h_ref[0, :, vv] = num / jnp.maximum(jnp.abs(den), jnp.exp(-mt))

        m_new = mt[last, :]
        b_last = b_col[last, :]
        w_state = jnp.exp(b_last - b_col + ig_col - m_new)
        decay = jnp.exp(b_last + m0 - m_new)
        kw = kh.astype(F32) * w_state
        c_new = decay * c0 + _dot_tn(kw.astype(BF16), vh)
        n_new = decay * n0 + jnp.sum(kw, axis=0, keepdims=True)
        c_sc[hh] = c_new
        n_sc[hh] = n_new
        m_sc[hh] = jnp.broadcast_to(m_new, (1, LANES))
        c_out_ref[0, hh] = c_new
        n_out_ref[0, hh] = n_new
        m_out_ref[0, hh] = jnp.broadcast_to(m_new, (1, LANES))


def _mlstm_prompt(qm, km, vm, gates, cum_l, gates_t, cum_lt):
    nb, lp, d_mqk = qm.shape
    d_mv = vm.shape[-1]
    nc = lp // CHUNK
    hm = N_MLSTM_HEADS
    blk = lambda w: pl.BlockSpec((1, CHUNK, w), lambda b, i: (b, i, 0))
    tblk = pl.BlockSpec((1, 1, LANES, CHUNK), lambda b, i: (b, i, 0, 0))
    return pl.pallas_call(
        _mlstm_kernel,
        grid=(nb, nc),
        in_specs=[blk(d_mqk), blk(d_mqk), blk(d_mv), blk(LANES), blk(LANES), tblk, tblk],
        out_specs=(blk(d_mv),
                   pl.BlockSpec((1, hm, MLSTM_QK_DIM, MLSTM_V_DIM), lambda b, i: (b, 0, 0, 0)),
                   pl.BlockSpec((1, hm, 1, MLSTM_QK_DIM), lambda b, i: (b, 0, 0, 0)),
                   pl.BlockSpec((1, hm, 1, LANES), lambda b, i: (b, 0, 0, 0))),
        out_shape=(jax.ShapeDtypeStruct((nb, lp, d_mv), F32),
                   jax.ShapeDtypeStruct((nb, hm, MLSTM_QK_DIM, MLSTM_V_DIM), F32),
                   jax.ShapeDtypeStruct((nb, hm, 1, MLSTM_QK_DIM), F32),
                   jax.ShapeDtypeStruct((nb, hm, 1, LANES), F32)),
        scratch_shapes=[pltpu.VMEM((hm, MLSTM_QK_DIM, MLSTM_V_DIM), F32),
                        pltpu.VMEM((hm, 1, MLSTM_QK_DIM), F32),
                        pltpu.VMEM((hm, 1, LANES), F32)],
        compiler_params=_cparams("parallel", "arbitrary"),
        name="mlstm_prompt",
    )(qm, km, vm, gates, cum_l, gates_t, cum_lt)


def _paged_kernel(pt_ref, q_ref, kn_ref, vn_ref, lfn_ref, *refs, pps):
    k_refs = refs[:pps]
    v_refs = refs[pps:2 * pps]
    lf_refs = refs[2 * pps:3 * pps]
    o_ref, m_sc, l_sc, acc_sc, carry_sc = refs[3 * pps:]
    g = pl.program_id(1)
    d_att = N_ATT_HEADS * ATT_HEAD_DIM
    head_of_lane = lax.broadcasted_iota(I32, (N_ATT_HEADS, d_att), 1) // ATT_HEAD_DIM
    head_of_row = lax.broadcasted_iota(I32, (N_ATT_HEADS, d_att), 0)
    own = head_of_lane == head_of_row
    q_bd = jnp.where(own, jnp.broadcast_to(q_ref[0].astype(F32), (N_ATT_HEADS, d_att)), 0.0)

    @pl.when(g == 0)
    def _():
        m_sc[...] = jnp.sum(q_bd * kn_ref[0], axis=-1, keepdims=True)
        l_sc[...] = jnp.ones(l_sc.shape, F32)
        acc_sc[...] = jnp.broadcast_to(vn_ref[0], acc_sc.shape)
        carry_sc[...] = jnp.zeros(carry_sc.shape, F32)

    q_bf = q_bd.astype(BF16)
    page = k_refs[0].shape[1]
    sp = lax.broadcasted_iota(I32, (page, 2 * page), 0)
    ss = lax.broadcasted_iota(I32, (page, 2 * page), 1)
    suffix_mat = jnp.where(jnp.logical_or(sp > ss, ss == page), 1.0, 0.0).astype(BF16)
    for u in range(pps):
        kp = k_refs[u][0].astype(BF16)
        vp = v_refs[u][0].astype(BF16)
        hi, mid, lo = _split3(lf_refs[u][0])
        sfx = _dot(hi, suffix_mat) + (_dot(mid, suffix_mat) + _dot(lo, suffix_mat))
        carry = carry_sc[...]
        s = _dot_nt(q_bf, kp) + (lfn_ref[0] + carry + sfx[:, :page])
        m_old = m_sc[...]
        m_new = jnp.maximum(m_old, jnp.max(s, axis=-1, keepdims=True))
        p = jnp.exp(s - m_new)
        alpha = jnp.exp(m_old - m_new)
        l_sc[...] = alpha * l_sc[...] + jnp.sum(p, axis=-1, keepdims=True)
        acc_sc[...] = alpha * acc_sc[...] + _dot(p.astype(BF16), vp)
        m_sc[...] = m_new
        carry_sc[...] = carry + sfx[:, page:page + 1]

    @pl.when(g == pl.num_programs(1) - 1)
    def _():
        for hh in range(N_ATT_HEADS):
            sl = slice(hh * ATT_HEAD_DIM, (hh + 1) * ATT_HEAD_DIM)
            o_ref[0, :, sl] = acc_sc[hh:hh + 1, sl] / l_sc[hh:hh + 1, :]


def _paged_attention(q, k_new, v_new, lf_new, cache_k, cache_v, cache_lft, page_table):
    db, n_pages = page_table.shape
    n_pool, page, d_att = cache_k.shape
    pps = PAGES_PER_STEP
    steps = n_pages // pps
    tok = lambda: pl.BlockSpec((1, 1, d_att), lambda b, g, pt: (b, 0, 0))

    def page_idx(u):
        return lambda b, g, pt: (pt[b * n_pages + (n_pages - 1 - (g * pps + u))], 0, 0)

    kv_specs = [pl.BlockSpec((1, page, d_att), page_idx(u)) for u in range(pps)]
    lf_specs = [pl.BlockSpec((1, N_ATT_HEADS, page), page_idx(u)) for u in range(pps)]
    grid_spec = pltpu.PrefetchScalarGridSpec(
        num_scalar_prefetch=1,
        grid=(db, steps),
        in_specs=[tok(), tok(), tok(), pl.BlockSpec((1, N_ATT_HEADS, 1), lambda b, g, pt: (b, 0, 0))]
        + kv_specs + kv_specs + lf_specs,
        out_specs=tok(),
        scratch_shapes=[pltpu.VMEM((N_ATT_HEADS, 1), F32), pltpu.VMEM((N_ATT_HEADS, 1), F32),
                        pltpu.VMEM((N_ATT_HEADS, d_att), F32), pltpu.VMEM((N_ATT_HEADS, 1), F32)],
    )
    return pl.pallas_call(
        functools.partial(_paged_kernel, pps=pps),
        grid_spec=grid_spec,
        out_shape=jax.ShapeDtypeStruct((db, 1, d_att), F32),
        compiler_params=_cparams("parallel", "arbitrary"),
        name="paged_fox",
    )(page_table.reshape(-1), q, k_new, v_new, lf_new, *([cache_k] * pps), *([cache_v] * pps), *([cache_lft] * pps))


def _mstep_kernel(q_ref, k_ref, v_ref, ig_ref, lf_ref, c_ref, n_ref, m_ref,
                  h_ref, c_out_ref, n_out_ref, m_out_ref):
    for hh in range(N_MLSTM_HEADS):
        q = q_ref[0, hh].astype(F32)
        k = k_ref[0, hh].astype(F32)
        v = v_ref[0, hh].astype(F32)
        ig = ig_ref[0, hh]
        lf = lf_ref[0, hh]
        c0 = c_ref[0, hh]
        n0 = n_ref[0, hh]
        m0 = m_ref[0, hh]
        inter = lf + m0
        m = jnp.maximum(inter, ig)
        w_intra = jnp.exp(ig - m)
        w_inter = jnp.exp(inter - m)
        s = jnp.sum(q * k, axis=0, keepdims=True) * w_intra
        num = s * v + w_inter * jnp.sum(q * c0, axis=0, keepdims=True)
        den = s + w_inter * jnp.sum(q * n0, axis=0, keepdims=True)
        h_ref[0, hh] = num / jnp.maximum(jnp.abs(den), jnp.exp(-m))
        c_out_ref[0, hh] = w_inter * c0 + w_intra * (k * v)
        n_out_ref[0, hh] = w_inter * n0 + w_intra * k
        m_out_ref[0, hh] = m


def _mlstm_step(q_col, k_col, v_row, ig, lf, state_c, state_n_col, state_m):
    db, hm, dk, dv = state_c.shape
    spec = lambda a, b_: pl.BlockSpec((1, hm, a, b_), lambda i: (i, 0, 0, 0))
    return pl.pallas_call(
        _mstep_kernel,
        grid=(db,),
        in_specs=[spec(dk, 1), spec(dk, 1), spec(1, dv), spec(1, 1), spec(1, 1), spec(dk, dv), spec(dk, 1), spec(1, 1)],
        out_specs=(spec(1, dv), spec(dk, dv), spec(dk, 1), spec(1, 1)),
        out_shape=(jax.ShapeDtypeStruct((db, hm, 1, dv), F32), jax.ShapeDtypeStruct((db, hm, dk, dv), F32),
                   jax.ShapeDtypeStruct((db, hm, dk, 1), F32), jax.ShapeDtypeStruct((db, hm, 1, 1), F32)),
        compiler_params=_cparams("parallel"),
        name="mlstm_step",
    )(q_col, k_col, v_row, ig, lf, state_c, state_n_col, state_m)


def _mix_kernel(att_ref, hm_ref, og_ref, h_ref, gm_ref, w_ref, lg_ref, lb_ref, wrh_ref, wrl_ref, br_ref,
                h1_ref, h1b_ref, e_ref, gate_ref, *, alpha):
    d_att = att_ref.shape[-1]
    parts = []
    for hh in range(N_MLSTM_HEADS):
        vv = slice(hh * MLSTM_V_DIM, (hh + 1) * MLSTM_V_DIM)
        x = hm_ref[:, vv]
        xn = x * lax.rsqrt(jnp.mean(x * x, axis=-1, keepdims=True) + RMS_EPS) * gm_ref[:, vv]
        parts.append((og_ref[:, vv] * xn).astype(BF16))
    mix = _dot(att_ref[...].astype(BF16), w_ref[0:d_att, :])
    for hh in range(N_MLSTM_HEADS):
        r0 = d_att + hh * MLSTM_V_DIM
        mix = mix + _dot(parts[hh], w_ref[r0:r0 + MLSTM_V_DIM, :])
    h1 = _layer_norm(alpha * h_ref[...] + mix, lg_ref[...], lb_ref[...])
    h1_ref[...] = h1
    h1b_ref[...] = h1.astype(BF16)

    logits = _dot3(h1, wrh_ref[...], wrl_ref[...]) + br_ref[...]
    lane = lax.broadcasted_iota(I32, logits.shape, 1).astype(F32)
    e_out = jnp.zeros(logits.shape, F32)
    top = []
    for kk in range(TOP_K):
        mx = jnp.max(logits, axis=-1, keepdims=True)
        idx = jnp.min(jnp.where(logits == mx, lane, float(LANES)), axis=-1, keepdims=True)
        logits = jnp.where(lane == idx, -jnp.inf, logits)
        e_out = jnp.where(lane == float(kk), idx, e_out)
        top.append(mx)
    ex = [jnp.exp(t - top[0]) for t in top]
    inv = 1.0 / (ex[0] + ex[1] + ex[2] + ex[3])
    gate = jnp.zeros(logits.shape, F32)
    for kk in range(TOP_K):
        gate = jnp.where(lane == float(kk), ex[kk] * inv, gate)
    e_ref[...] = e_out.astype(I32)
    gate_ref[...] = gate


def _mix(att, hm, og, h, prm, tm, alpha):
    rows, d = h.shape
    d_att = att.shape[-1]
    d_mv = hm.shape[-1]
    row = lambda w: pl.BlockSpec((tm, w), lambda i: (i, 0))
    const = lambda r, c: pl.BlockSpec((r, c), lambda i: (0, 0), pipeline_mode=pl.Buffered(1))
    return pl.pallas_call(
        functools.partial(_mix_kernel, alpha=alpha),
        grid=(rows // tm,),
        in_specs=[row(d_att), row(d_mv), row(d_mv), row(d), const(1, d_mv), const(d_att + d_mv, d),
                  const(1, d), const(1, d), const(d, LANES), const(d, LANES), const(1, LANES)],
        out_specs=(row(d), row(d), row(LANES), row(LANES)),
        out_shape=(jax.ShapeDtypeStruct((rows, d), F32), jax.ShapeDtypeStruct((rows, d), BF16),
                   jax.ShapeDtypeStruct((rows, LANES), I32), jax.ShapeDtypeStruct((rows, LANES), F32)),
        compiler_params=_cparams("parallel"),
        name="mix",
    )(att, hm, og, h, prm["g_m_out"], prm["w_out"], prm["ln1_g"], prm["ln1_b"], prm["wr_hi"], prm["wr_lo"],
      prm["b_router"])


def _moe_kernel(ie_ref, ns_ref, ib_ref, iv_ref, x_ref, wg_ref, wu_ref, bg_ref, bu_ref, wd_ref, bd_ref, o_ref):
    i = pl.program_id(0)
    f = pl.program_id(1)
    n = ns_ref[i]

    @pl.when(jnp.logical_and(n > 0, f == 0))
    def _():
        o_ref[...] = jnp.broadcast_to(bd_ref[0], o_ref.shape)

    for nn in range(1, MOE_ROWS // MOE_SUB + 1):
        @pl.when(n == nn)
        def _():
            rows = slice(0, nn * MOE_SUB)
            xs = x_ref[rows, :]
            gp = _dot(xs, wg_ref[0].astype(BF16)) + bg_ref[0]
            up = _dot(xs, wu_ref[0].astype(BF16)) + bu_ref[0]
            gp = jnp.minimum(gp, SWIGLU_LIMIT)
            up = jnp.clip(up, -SWIGLU_LIMIT, SWIGLU_LIMIT)
            act = gp * _sigmoid(SWIGLU_ALPHA * gp) * (up + 1.0)
            o_ref[rows, :] += _dot(act.astype(BF16), wd_ref[0].astype(BF16))


def _moe(x_sorted, item_e, item_ns, item_blk, item_valid, w_gate_up, b_gate_up, w_down, b_down):
    n_slots, d = x_sorted.shape
    n_exp, _, two_f = w_gate_up.shape
    d_ff = two_f // 2
    nf = d_ff // MOE_FT
    n_items = n_slots // MOE_ROWS

    def ftile(i, f, iv):
        return f * iv[i] + (nf - 1) * (1 - iv[i])

    grid_spec = pltpu.PrefetchScalarGridSpec(
        num_scalar_prefetch=4,
        grid=(n_items, nf),
        in_specs=[
            pl.BlockSpec((MOE_ROWS, d), lambda i, f, ie, ns, ib, iv: (ib[i], 0)),
            pl.BlockSpec((1, d, MOE_FT), lambda i, f, ie, ns, ib, iv: (ie[i], 0, ftile(i, f, iv))),
            pl.BlockSpec((1, d, MOE_FT), lambda i, f, ie, ns, ib, iv: (ie[i], 0, nf + ftile(i, f, iv))),
            pl.BlockSpec((1, 1, MOE_FT), lambda i, f, ie, ns, ib, iv: (ie[i], 0, ftile(i, f, iv))),
            pl.BlockSpec((1, 1, MOE_FT), lambda i, f, ie, ns, ib, iv: (ie[i], 0, nf + ftile(i, f, iv))),
            pl.BlockSpec((1, MOE_FT, d), lambda i, f, ie, ns, ib, iv: (ie[i], ftile(i, f, iv), 0)),
            pl.BlockSpec((1, 1, d), lambda i, f, ie, ns, ib, iv: (ie[i], 0, 0)),
        ],
        out_specs=pl.BlockSpec((MOE_ROWS, d), lambda i, f, ie, ns, ib, iv: (ib[i], 0)),
    )
    return pl.pallas_call(
        _moe_kernel,
        grid_spec=grid_spec,
        out_shape=jax.ShapeDtypeStruct((n_slots, d), F32),
        compiler_params=_cparams("arbitrary", "arbitrary"),
        name="moe",
    )(item_e, item_ns, item_blk, item_valid, x_sorted, w_gate_up, w_gate_up,
      b_gate_up.reshape(n_exp, 1, two_f), b_gate_up.reshape(n_exp, 1, two_f), w_down, b_down.reshape(n_exp, 1, d))


def _combine_kernel(y_ref, gate_ref, h1_ref, lg_ref, lb_ref, o_ref, *, alpha):
    acc = alpha * h1_ref[...]
    for kk in range(TOP_K):
        acc = acc + y_ref[kk] * gate_ref[:, kk:kk + 1]
    o_ref[...] = _layer_norm(acc, lg_ref[...], lb_ref[...])


def _combine(y_pairs, gate, h1, ln_g, ln_b, tm, alpha):
    rows, d = h1.shape
    const = lambda r, c: pl.BlockSpec((r, c), lambda i: (0, 0))
    return pl.pallas_call(
        functools.partial(_combine_kernel, alpha=alpha),
        grid=(pl.cdiv(rows, tm),),
        in_specs=[pl.BlockSpec((TOP_K, tm, d), lambda i: (0, i, 0)), pl.BlockSpec((tm, LANES), lambda i: (i, 0)),
                  pl.BlockSpec((tm, d), lambda i: (i, 0)), const(1, d), const(1, d)],
        out_specs=pl.BlockSpec((tm, d), lambda i: (i, 0)),
        out_shape=jax.ShapeDtypeStruct((rows, d), F32),
        compiler_params=_cparams("parallel"),
        name="combine",
    )(y_pairs, gate, h1, ln_g, ln_b)


def _route(top_e, row_valid):
    n_rows = top_e.shape[0]
    n_pairs = n_rows * TOP_K
    pair_e = jnp.where(row_valid[:, None], top_e, N_EXPERTS).reshape(-1)
    order = jnp.argsort(pair_e, stable=True).astype(I32)
    e_sorted = pair_e[order]
    counts = jnp.bincount(pair_e, length=N_EXPERTS + 1).astype(I32)
    padded = (counts[:N_EXPERTS] + MOE_ROWS - 1) // MOE_ROWS * MOE_ROWS
    pad_end = jnp.cumsum(padded)
    pad_start = pad_end - padded
    start = jnp.cumsum(counts) - counts
    n_items = n_pairs // MOE_ROWS + N_EXPERTS
    n_slots = n_items * MOE_ROWS
    pad_start_ext = jnp.concatenate([pad_start, jnp.full((1,), n_slots, I32)])
    slot_sorted = pad_start_ext[e_sorted] + jnp.arange(n_pairs, dtype=I32) - start[e_sorted]
    slot_tok = jnp.zeros((n_slots,), I32).at[slot_sorted].set(order // TOP_K, mode="drop")
    pair_slot = jnp.zeros((n_pairs,), I32).at[order].set(jnp.where(e_sorted < N_EXPERTS, slot_sorted, 0))
    item_start = jnp.arange(n_items, dtype=I32) * MOE_ROWS
    n_valid = pad_end[-1] // MOE_ROWS
    item_valid = (jnp.arange(n_items) < n_valid).astype(I32)
    last = jnp.maximum(n_valid - 1, 0)
    item_blk = jnp.minimum(jnp.arange(n_items, dtype=I32), last)
    item_e = jnp.minimum(jnp.searchsorted(pad_end, item_blk * MOE_ROWS, side="right"), N_EXPERTS - 1).astype(I32)
    rows_left = pad_start[item_e] + counts[item_e] - item_start
    item_ns = jnp.clip((rows_left + MOE_SUB - 1) // MOE_SUB, 0, MOE_ROWS // MOE_SUB) * item_valid
    return slot_tok, pair_slot, item_e, item_ns.astype(I32), item_blk, item_valid


def _prep_params(ln_in_g, ln_in_b, w_in, b_att_f, g_att_q, g_att_k, b_m_i, b_m_f, g_m_out, w_out,
                 ln1_g, ln1_b, w_router, b_router, ln2_g, ln2_b):
    d = w_in.shape[0]
    d_att = N_ATT_HEADS * ATT_HEAD_DIM
    d_mqk = N_MLSTM_HEADS * MLSTM_QK_DIM
    d_mv = N_MLSTM_HEADS * MLSTM_V_DIM
    sizes = (d_att, d_att, d_att, N_ATT_HEADS, d_mqk, d_mqk, d_mv, N_MLSTM_HEADS, N_MLSTM_HEADS, d_mv)
    offs = [0]
    for s in sizes:
        offs.append(offs[-1] + s)
    cols = lambda i: w_in[:, offs[i]:offs[i + 1]]
    w_main = jnp.concatenate([cols(0), cols(1), cols(2), cols(4), cols(5), cols(6), cols(9)], axis=1).astype(BF16)
    w_gate = jnp.concatenate([cols(3), cols(7), cols(8), jnp.zeros((d, LANES - N_GATES), F32)], axis=1)
    wg_hi, wg_lo = _split2(w_gate)
    gate_bias = jnp.concatenate([b_att_f, b_m_i, b_m_f, jnp.zeros((LANES - N_GATES,), F32)]).reshape(1, LANES)
    w_r = jnp.concatenate([w_router, jnp.zeros((d, LANES - N_EXPERTS), F32)], axis=1)
    wr_hi, wr_lo = _split2(w_r)
    b_r = jnp.concatenate([b_router, jnp.full((LANES - N_EXPERTS,), NEG, F32)]).reshape(1, LANES)
    return dict(
        ln_in_g=ln_in_g.reshape(1, d), ln_in_b=ln_in_b.reshape(1, d), w_main=w_main, wg_hi=wg_hi, wg_lo=wg_lo,
        gate_bias=gate_bias, g_att_q=g_att_q.reshape(1, -1), g_att_k=g_att_k.reshape(1, -1),
        g_m_out=g_m_out.reshape(1, -1), w_out=w_out.astype(BF16), ln1_g=ln1_g.reshape(1, d),
        ln1_b=ln1_b.reshape(1, d), wr_hi=wr_hi, wr_lo=wr_lo, b_router=b_r,
        ln2_g=ln2_g.reshape(1, d), ln2_b=ln2_b.reshape(1, d))


def kernel(x_prompt, x_sample, cache_k, cache_v, cache_logf, state_C, state_n, state_m, page_table,
           meta_tokens, ln_in_g, ln_in_b, w_in, b_att_f, g_att_q, g_att_k, b_m_i, b_m_f, g_m_out, w_out,
           ln1_g, ln1_b, w_router, b_router, w_gate_up, b_gate_up, w_down, b_down, ln2_g, ln2_b):
    depth = w_in.shape[0]
    assert depth == 1, "single-layer trunk only"
    nb, seq, d = x_prompt.shape
    db = x_sample.shape[0]
    assert x_sample.shape[1] == 1, "one new token per sample"
    alpha = (2.0 * depth) ** 0.25
    d_att = N_ATT_HEADS * ATT_HEAD_DIM
    hm = N_MLSTM_HEADS
    prm = _prep_params(ln_in_g, ln_in_b, w_in[0], b_att_f[0], g_att_q[0], g_att_k[0], b_m_i[0], b_m_f[0],
                       g_m_out[0], w_out[0], ln1_g[0], ln1_b[0], w_router[0], b_router[0], ln2_g[0], ln2_b[0])

    length = N_META + seq
    lp = pl.cdiv(length, CHUNK) * CHUNK
    meta = jnp.broadcast_to(meta_tokens[None], (nb, N_META, d))
    xp = jnp.concatenate([meta, x_prompt, jnp.zeros((nb, lp - length, d), F32)], axis=1)
    (hp, qa, k32, kb, v32, vb, gates, qm, km, vm, og) = _proj(xp, length, lp // 8, prm)
    cum_g, cum_l, cum_gt, cum_lt, gates_t = _gate_scan(gates)
    att_p = _fox_prompt(qa, kb, vb, cum_g, cum_gt)
    hm_p, c_p, n_p, m_p = _mlstm_prompt(qm, km, vm, gates, cum_l, gates_t, cum_lt)
    rows_p = nb * lp
    h1_p, h1b_p, e_p, gate_p = _mix(att_p.reshape(rows_p, d_att), hm_p.reshape(rows_p, -1), og.reshape(rows_p, -1),
                                    hp.reshape(rows_p, d), prm, 256, alpha)

    xs = x_sample.reshape(1, db, d)
    (hs, qa_s, k32_s, _, v32_s, _, gates_s, qm_s, km_s, vm_s, og_s) = _proj(xs, db, db, prm)
    n_pool, page = cache_k.shape[1], cache_k.shape[2]
    lf_new = gates_s[0, :, GATE_LF_ATT:GATE_LF_ATT + N_ATT_HEADS].reshape(db, N_ATT_HEADS, 1)
    att_s = _paged_attention(
        qa_s.reshape(db, 1, d_att), k32_s.reshape(db, 1, d_att), v32_s.reshape(db, 1, d_att), lf_new,
        cache_k[0].reshape(n_pool, page, d_att), cache_v[0].reshape(n_pool, page, d_att),
        jnp.swapaxes(cache_logf[0], 1, 2), page_table)
    ig_s = gates_s[0, :, GATE_IG:GATE_IG + hm].reshape(db, hm, 1, 1)
    lfm_s = gates_s[0, :, GATE_LF_M:GATE_LF_M + hm].reshape(db, hm, 1, 1)
    hm_s, c_s, n_s, m_s = _mlstm_step(
        qm_s.astype(F32).reshape(db, hm, MLSTM_QK_DIM, 1), km_s.astype(F32).reshape(db, hm, MLSTM_QK_DIM, 1),
        vm_s.astype(F32).reshape(db, hm, 1, MLSTM_V_DIM), ig_s, lfm_s, state_C[0],
        state_n[0].reshape(db, hm, MLSTM_QK_DIM, 1), state_m[0].reshape(db, hm, 1, 1))
    h1_s, h1b_s, e_s, gate_s = _mix(att_s.reshape(db, d_att), hm_s.reshape(db, -1), og_s.reshape(db, -1),
                                    hs.reshape(db, d), prm, db, alpha)

    h1 = jnp.concatenate([h1_p, h1_s], axis=0)
    h1b = jnp.concatenate([h1b_p, h1b_s], axis=0)
    top_e = jnp.concatenate([e_p, e_s], axis=0)[:, :TOP_K]
    gate = jnp.concatenate([gate_p, gate_s], axis=0)
    n_rows = rows_p + db
    pos = jnp.arange(rows_p, dtype=I32) % lp
    row_valid = jnp.concatenate([pos < length, jnp.ones((db,), bool)])
    slot_tok, pair_slot, item_e, item_ns, item_blk, item_valid = _route(top_e, row_valid)
    x_sorted = jnp.take(h1b, slot_tok, axis=0)
    y_sorted = _moe(x_sorted, item_e, item_ns, item_blk, item_valid, w_gate_up[0], b_gate_up[0], w_down[0], b_down[0])
    y_pairs = jnp.take(y_sorted, pair_slot.reshape(n_rows, TOP_K).T, axis=0)
    out = _combine(y_pairs, gate, h1, prm["ln2_g"], prm["ln2_b"], 256, alpha)

    y_prompt = out[:rows_p].reshape(nb, lp, d)[:, N_META:length]
    y_sample = out[rows_p:].reshape(db, 1, d)
    k_prompt = k32.reshape(1, nb, length, N_ATT_HEADS, ATT_HEAD_DIM)
    v_prompt = v32.reshape(1, nb, length, N_ATT_HEADS, ATT_HEAD_DIM)
    lf_prompt = gates[:, :length, GATE_LF_ATT:GATE_LF_ATT + N_ATT_HEADS][None]
    return (y_prompt, y_sample, k_prompt, v_prompt, lf_prompt,
            c_p[None], n_p.reshape(1, nb, hm, MLSTM_QK_DIM), m_p[:, :, 0, 0][None],
            k32_s.reshape(1, db, 1, N_ATT_HEADS, ATT_HEAD_DIM), v32_s.reshape(1, db, 1, N_ATT_HEADS, ATT_HEAD_DIM),
            gates_s[0, :, GATE_LF_ATT:GATE_LF_ATT + N_ATT_HEADS].reshape(1, db, 1, N_ATT_HEADS),
            c_s[None], n_s.reshape(1, db, hm, MLSTM_QK_DIM), m_s.reshape(1, db, hm))
```

```python
import functools

import jax
import jax.numpy as jnp
from jax import lax
from jax.experimental import pallas as pl
from jax.experimental.pallas import tpu as pltpu

F32 = jnp.float32
BF16 = jnp.bfloat16
I32 = jnp.int32

N_META = 16
CHUNK = 128
ATT_HEAD_DIM = 128
N_ATT_HEADS = 8
N_MLSTM_HEADS = 4
MLSTM_QK_DIM = 128
MLSTM_V_DIM = 256
N_EXPERTS = 32
TOP_K = 4
LOG2_TOP_K = 2
SWIGLU_ALPHA = 1.702
SWIGLU_LIMIT = 7.0
LN_EPS = 1e-5
RMS_EPS = 1e-6
NEG = -1e30

LANES = 128
GATE_LF_ATT = 0
GATE_IG = N_ATT_HEADS
GATE_LF_M = GATE_IG + N_MLSTM_HEADS
N_GATES = GATE_LF_M + N_MLSTM_HEADS

VMEM_LIMIT = 56 * 1024 * 1024

MOE_ROWS = 512
MOE_SUB = 128
MOE_FT = 512
PAGES_PER_STEP = 8
FOX_TQ = 256


def _cparams(*sem):
    return pltpu.CompilerParams(dimension_semantics=sem, vmem_limit_bytes=VMEM_LIMIT)


def _split2(x):
    hi = x.astype(BF16)
    lo = (x - hi.astype(F32)).astype(BF16)
    return hi, lo


def _split3(x):
    hi = x.astype(BF16)
    r1 = x - hi.astype(F32)
    mid = r1.astype(BF16)
    lo = (r1 - mid.astype(F32)).astype(BF16)
    return hi, mid, lo


def _dot(a, b):
    return jnp.dot(a, b, preferred_element_type=F32)


def _dot_nt(a, b):
    return lax.dot_general(a, b, (((1,), (1,)), ((), ())), preferred_element_type=F32)


def _dot_tn(a, b):
    return lax.dot_general(a, b, (((0,), (0,)), ((), ())), preferred_element_type=F32)


def _dot3(x_f32, w_hi, w_lo):
    x_hi, x_lo = _split2(x_f32)
    return _dot(x_hi, w_hi) + (_dot(x_lo, w_hi) + _dot(x_hi, w_lo))


def _layer_norm(x, g, b):
    mu = jnp.mean(x, axis=-1, keepdims=True)
    xc = x - mu
    var = jnp.mean(xc * xc, axis=-1, keepdims=True)
    return xc * lax.rsqrt(var + LN_EPS) * g + b


def _log_sigmoid(z):
    return jnp.minimum(z, 0.0) - jnp.log(1.0 + jnp.exp(-jnp.abs(z)))


def _sigmoid(z):
    return 1.0 / (1.0 + jnp.exp(-z))


def _proj_kernel(x_ref, lng_ref, lnb_ref, w_ref, wgh_ref, wgl_ref, gb_ref, gq_ref, gk_ref,
                 h_ref, q_ref, k32_ref, kb_ref, v32_ref, vb_ref, g_ref, qm_ref, km_ref, vm_ref, og_ref,
                 *, tm, valid_len):
    d_att = N_ATT_HEADS * ATT_HEAD_DIM
    d_mqk = N_MLSTM_HEADS * MLSTM_QK_DIM
    d_mv = N_MLSTM_HEADS * MLSTM_V_DIM
    h = _layer_norm(x_ref[0], lng_ref[...], lnb_ref[...])
    h_ref[0] = h
    hb = h.astype(BF16)

    def seg(c0, width):
        return _dot(hb, w_ref[:, c0:c0 + width])

    scale = ATT_HEAD_DIM ** -0.5
    qa = seg(0, d_att)
    ka = seg(d_att, d_att)
    for hh in range(N_ATT_HEADS):
        sl = slice(hh * ATT_HEAD_DIM, (hh + 1) * ATT_HEAD_DIM)
        qh = qa[:, sl]
        qn = qh * lax.rsqrt(jnp.mean(qh * qh, axis=-1, keepdims=True) + RMS_EPS) * gq_ref[...]
        q_ref[0, :, sl] = (qn * scale).astype(BF16)
        kh = ka[:, sl]
        kn = kh * lax.rsqrt(jnp.mean(kh * kh, axis=-1, keepdims=True) + RMS_EPS) * gk_ref[...]
        k32_ref[0, :, sl] = kn
        kb_ref[0, :, sl] = kn.astype(BF16)
    va = seg(2 * d_att, d_att)
    v32_ref[0] = va
    vb_ref[0] = va.astype(BF16)
    c0 = 3 * d_att
    qm_ref[0] = seg(c0, d_mqk).astype(BF16)
    km_ref[0] = (seg(c0 + d_mqk, d_mqk) * (MLSTM_QK_DIM ** -0.5)).astype(BF16)
    vm_ref[0] = seg(c0 + 2 * d_mqk, d_mv).astype(BF16)
    og_ref[0] = _sigmoid(seg(c0 + 2 * d_mqk + d_mv, d_mv))

    z = _dot3(h, wgh_ref[...], wgl_ref[...]) + gb_ref[...]
    lane = lax.broadcasted_iota(I32, z.shape, 1)
    row = pl.program_id(1) * tm + lax.broadcasted_iota(I32, z.shape, 0)
    is_ig = jnp.logical_and(lane >= GATE_IG, lane < GATE_LF_M)
    gates = jnp.where(is_ig, z, jnp.where(lane < N_GATES, _log_sigmoid(z), 0.0))
    gates = jnp.where(row < valid_len, gates, jnp.where(is_ig, NEG, 0.0))
    g_ref[0] = gates


def _proj(x, valid_len, tm, prm):
    nb, lp, d = x.shape
    d_att = N_ATT_HEADS * ATT_HEAD_DIM
    d_mqk = N_MLSTM_HEADS * MLSTM_QK_DIM
    d_mv = N_MLSTM_HEADS * MLSTM_V_DIM
    n_main = prm["w_main"].shape[1]
    grid = (nb, lp // tm)
    row = lambda w: pl.BlockSpec((1, tm, w), lambda b, i: (b, i, 0))
    const = lambda r, c: pl.BlockSpec((r, c), lambda b, i: (0, 0), pipeline_mode=pl.Buffered(1))
    out_shape = (
        jax.ShapeDtypeStruct((nb, lp, d), F32),
        jax.ShapeDtypeStruct((nb, lp, d_att), BF16),
        jax.ShapeDtypeStruct((nb, valid_len, d_att), F32),
        jax.ShapeDtypeStruct((nb, lp, d_att), BF16),
        jax.ShapeDtypeStruct((nb, valid_len, d_att), F32),
        jax.ShapeDtypeStruct((nb, lp, d_att), BF16),
        jax.ShapeDtypeStruct((nb, lp, LANES), F32),
        jax.ShapeDtypeStruct((nb, lp, d_mqk), BF16),
        jax.ShapeDtypeStruct((nb, lp, d_mqk), BF16),
        jax.ShapeDtypeStruct((nb, lp, d_mv), BF16),
        jax.ShapeDtypeStruct((nb, lp, d_mv), F32),
    )
    out_specs = (row(d), row(d_att), row(d_att), row(d_att), row(d_att), row(d_att), row(LANES),
                 row(d_mqk), row(d_mqk), row(d_mv), row(d_mv))
    return pl.pallas_call(
        functools.partial(_proj_kernel, tm=tm, valid_len=valid_len),
        grid=grid,
        in_specs=[row(d), const(1, d), const(1, d), const(d, n_main), const(d, LANES), const(d, LANES),
                  const(1, LANES), const(1, ATT_HEAD_DIM), const(1, ATT_HEAD_DIM)],
        out_specs=out_specs,
        out_shape=out_shape,
        compiler_params=_cparams("parallel", "arbitrary"),
        name="proj",
    )(x, prm["ln_in_g"], prm["ln_in_b"], prm["w_main"], prm["wg_hi"], prm["wg_lo"], prm["gate_bias"],
      prm["g_att_q"], prm["g_att_k"])


def _gate_scan_kernel(g_ref, cl_ref, clt_ref, gt_ref, ka_ref, qa_ref, *, n_chunks):
    r = lax.broadcasted_iota(I32, (CHUNK, CHUNK), 0)
    c = lax.broadcasted_iota(I32, (CHUNK, CHUNK), 1)
    tri = jnp.where(c <= r, 1.0, 0.0).astype(BF16)
    carry = jnp.zeros((1, LANES), F32)
    for ci in range(n_chunks):
        sl = slice(ci * CHUNK, (ci + 1) * CHUNK)
        g = g_ref[0, sl, :]
        hi, mid, lo = _split3(g)
        loc = _dot(tri, hi) + (_dot(tri, mid) + _dot(tri, lo))
        glob = loc + carry
        carry = glob[CHUNK - 1:CHUNK, :]
        cl_ref[0, sl, :] = loc
        clt_ref[0, ci] = loc.T
        gt_ref[0, ci] = g.T
        for hh in range(N_ATT_HEADS):
            cum = glob[:, GATE_LF_ATT + hh:GATE_LF_ATT + hh + 1]
            t0 = cum.astype(BF16).astype(F32)
            r1 = cum - t0
            t1 = r1.astype(BF16).astype(F32)
            t2 = (r1 - t1).astype(BF16).astype(F32)
            hs = slice(hh * ATT_HEAD_DIM, (hh + 1) * ATT_HEAD_DIM)
            ones = jnp.where(c < 3, 1.0, 0.0)
            terms = jnp.where(c == 3, t0, jnp.where(c == 4, t1, jnp.where(c == 5, t2, 0.0)))
            qa_ref[0, sl, hs] = (ones + terms).astype(BF16)
            ones_k = jnp.where(jnp.logical_and(c >= 3, c < 6), 1.0, 0.0)
            terms_k = jnp.where(c == 0, t0, jnp.where(c == 1, t1, jnp.where(c == 2, t2, 0.0)))
            ka_ref[0, sl, hs] = (ones_k - terms_k).astype(BF16)


def _gate_scan(gates):
    nb, lp, _ = gates.shape
    nc = lp // CHUNK
    d_att = N_ATT_HEADS * ATT_HEAD_DIM
    col = pl.BlockSpec((1, lp, LANES), lambda b: (b, 0, 0))
    rowt = pl.BlockSpec((1, nc, LANES, CHUNK), lambda b: (b, 0, 0, 0))
    aug = pl.BlockSpec((1, lp, d_att), lambda b: (b, 0, 0))
    return pl.pallas_call(
        functools.partial(_gate_scan_kernel, n_chunks=nc),
        grid=(nb,),
        in_specs=[col],
        out_specs=(col, rowt, rowt, aug, aug),
        out_shape=(jax.ShapeDtypeStruct((nb, lp, LANES), F32),
                   jax.ShapeDtypeStruct((nb, nc, LANES, CHUNK), F32),
                   jax.ShapeDtypeStruct((nb, nc, LANES, CHUNK), F32),
                   jax.ShapeDtypeStruct((nb, lp, d_att), BF16),
                   jax.ShapeDtypeStruct((nb, lp, d_att), BF16)),
        compiler_params=_cparams("parallel"),
        name="gate_scan",
    )(gates)


def _fox_kernel(q_ref, qa_ref, k_ref, ka_ref, v_ref, o_ref, kx_sc, vt_sc, qx_sc, m_sc, l_sc, acc_sc,
                *, n_chunks, tq):
    ti = pl.program_id(1)
    dh = ATT_HEAD_DIM

    @pl.when(ti == 0)
    def _():
        for j in range(n_chunks):
            rows = slice(j * CHUNK, (j + 1) * CHUNK)
            for hh in range(N_ATT_HEADS):
                sl = slice(hh * dh, (hh + 1) * dh)
                kx_sc[j, :, 2 * hh * dh:(2 * hh + 1) * dh] = k_ref[0, rows, sl]
                kx_sc[j, :, (2 * hh + 1) * dh:(2 * hh + 2) * dh] = ka_ref[0, rows, sl]
                vt_sc[hh, j] = v_ref[0, rows, sl].astype(F32).T.astype(BF16)

    for hh in range(N_ATT_HEADS):
        sl = slice(hh * dh, (hh + 1) * dh)
        qx_sc[:, 2 * hh * dh:(2 * hh + 1) * dh] = q_ref[0, :, sl]
        qx_sc[:, (2 * hh + 1) * dh:(2 * hh + 2) * dh] = qa_ref[0, :, sl]
    m_sc[...] = jnp.full(m_sc.shape, NEG, F32)
    l_sc[...] = jnp.zeros(l_sc.shape, F32)
    acc_sc[...] = jnp.zeros(acc_sc.shape, F32)
    key = lax.broadcasted_iota(I32, (CHUNK, tq), 0)
    qry = lax.broadcasted_iota(I32, (CHUNK, tq), 1)

    def step(j, masked):
        scores = []
        for hh in range(N_ATT_HEADS):
            xs = slice(2 * hh * dh, (2 * hh + 2) * dh)
            scores.append(_dot_nt(kx_sc[j, :, xs], qx_sc[:, xs]))
        probs, alphas = [], []
        for hh in range(N_ATT_HEADS):
            s = scores[hh]
            if masked:
                s = jnp.where(j * CHUNK + key <= ti * tq + qry, s, NEG)
            m_old = m_sc[hh]
            m_new = jnp.maximum(m_old, jnp.max(s, axis=0, keepdims=True))
            p = jnp.exp(s - m_new)
            alpha = jnp.exp(m_old - m_new)
            l_sc[hh] = alpha * l_sc[hh] + jnp.sum(p, axis=0, keepdims=True)
            m_sc[hh] = m_new
            probs.append(p.astype(BF16))
            alphas.append(alpha)
        for hh in range(N_ATT_HEADS):
            acc_sc[hh] = alphas[hh] * acc_sc[hh] + _dot(vt_sc[hh, j], probs[hh])

    def body(j, carry):
        step(j, False)
        return carry

    per = tq // CHUNK
    lax.fori_loop(0, ti * per, body, 0)
    for dj in range(per):
        j = ti * per + dj

        @pl.when(j < n_chunks)
        def _():
            step(j, True)

    for hh in range(N_ATT_HEADS):
        sl = slice(hh * dh, (hh + 1) * dh)
        o_ref[0, :, sl] = (acc_sc[hh] / l_sc[hh]).T.astype(o_ref.dtype)


def _fox_prompt(q, q_aug, kb, k_aug, vb):
    nb, lp, d_att = q.shape
    nc = lp // CHUNK
    tq = FOX_TQ
    blk = pl.BlockSpec((1, tq, d_att), lambda b, i: (b, i, 0))
    full = pl.BlockSpec((1, lp, d_att), lambda b, i: (b, 0, 0))
    nh, dh = N_ATT_HEADS, ATT_HEAD_DIM
    return pl.pallas_call(
        functools.partial(_fox_kernel, n_chunks=nc, tq=tq),
        grid=(nb, pl.cdiv(lp, tq)),
        in_specs=[blk, blk, full, full, full],
        out_specs=blk,
        out_shape=jax.ShapeDtypeStruct((nb, lp, d_att), BF16),
        scratch_shapes=[pltpu.VMEM((nc, CHUNK, 2 * d_att), BF16),
                        pltpu.VMEM((nh, nc, dh, CHUNK), BF16),
                        pltpu.VMEM((tq, 2 * d_att), BF16),
                        pltpu.VMEM((nh, 1, tq), F32), pltpu.VMEM((nh, 1, tq), F32),
                        pltpu.VMEM((nh, dh, tq), F32)],
        compiler_params=_cparams("parallel", "arbitrary"),
        name="fox_prompt",
    )(q, q_aug, kb, k_aug, vb)


def _mlstm_kernel(q_ref, k_ref, v_ref, g_ref, cl_ref, gt_ref, clt_ref,
                  h_ref, c_out_ref, n_out_ref, m_out_ref, c_sc, n_sc, m_sc):
    ci = pl.program_id(1)

    @pl.when(ci == 0)
    def _():
        c_sc[...] = jnp.zeros(c_sc.shape, F32)
        n_sc[...] = jnp.zeros(n_sc.shape, F32)
        m_sc[...] = jnp.zeros(m_sc.shape, F32)

    r = lax.broadcasted_iota(I32, (CHUNK, CHUNK), 0)
    c = lax.broadcasted_iota(I32, (CHUNK, CHUNK), 1)
    causal = c <= r
    last = slice(CHUNK - 1, CHUNK)
    for hh in range(N_MLSTM_HEADS):
        qk = slice(hh * MLSTM_QK_DIM, (hh + 1) * MLSTM_QK_DIM)
        vv = slice(hh * MLSTM_V_DIM, (hh + 1) * MLSTM_V_DIM)
        ig_l, lf_l = GATE_IG + hh, GATE_LF_M + hh
        b_col = cl_ref[0, :, lf_l:lf_l + 1]
        ig_col = g_ref[0, :, ig_l:ig_l + 1]
        b_row = clt_ref[0, 0, lf_l:lf_l + 1, :]
        ig_row = gt_ref[0, 0, ig_l:ig_l + 1, :]
        m0 = m_sc[hh][:, 0:1]
        c0 = c_sc[hh]
        n0 = n_sc[hh]
        qh = q_ref[0, :, qk]
        kh = k_ref[0, :, qk]
        vh = v_ref[0, :, vv]

        dmat = jnp.where(causal, b_col - b_row + ig_row, NEG)
        inter = b_col + m0
        mt = jnp.maximum(inter, jnp.max(dmat, axis=-1, keepdims=True))
        w_intra = jnp.exp(dmat - mt)
        w_inter = jnp.exp(inter - mt)
        s = _dot_nt(qh, kh) * w_intra
        num = _dot(s.astype(BF16), vh) + w_inter * _dot(qh, c0.astype(BF16))
        den = jnp.sum(s, axis=-1, keepdims=True) + w_inter * jnp.sum(qh.astype(F32) * n0, axis=-1, keepdims=True)
        h_ref[0, :, vv] = num / jnp.maximum(jnp.abs(den), jnp.exp(-mt))

        m_new = mt[last, :]
        b_last = b_col[last, :]
        w_state = jnp.exp(b_last - b_col + ig_col - m_new)
        decay = jnp.exp(b_last + m0 - m_new)
        kw = kh.astype(F32) * w_state
        c_new = decay * c0 + _dot_tn(kw.astype(BF16), vh)
        n_new = decay * n0 + jnp.sum(kw, axis=0, keepdims=True)
        c_sc[hh] = c_new
        n_sc[hh] = n_new
        m_sc[hh] = jnp.broadcast_to(m_new, (1, LANES))
        c_out_ref[0, hh] = c_new
        n_out_ref[0, hh] = n_new
        m_out_ref[0, hh] = jnp.broadcast_to(m_new, (1, LANES))


def _mlstm_prompt(qm, km, vm, gates, cum_l, gates_t, cum_lt):
    nb, lp, d_mqk = qm.shape
    d_mv = vm.shape[-1]
    nc = lp // CHUNK
    hm = N_MLSTM_HEADS
    blk = lambda w: pl.BlockSpec((1, CHUNK, w), lambda b, i: (b, i, 0))
    tblk = pl.BlockSpec((1, 1, LANES, CHUNK), lambda b, i: (b, i, 0, 0))
    return pl.pallas_call(
        _mlstm_kernel,
        grid=(nb, nc),
        in_specs=[blk(d_mqk), blk(d_mqk), blk(d_mv), blk(LANES), blk(LANES), tblk, tblk],
        out_specs=(blk(d_mv),
                   pl.BlockSpec((1, hm, MLSTM_QK_DIM, MLSTM_V_DIM), lambda b, i: (b, 0, 0, 0)),
                   pl.BlockSpec((1, hm, 1, MLSTM_QK_DIM), lambda b, i: (b, 0, 0, 0)),
                   pl.BlockSpec((1, hm, 1, LANES), lambda b, i: (b, 0, 0, 0))),
        out_shape=(jax.ShapeDtypeStruct((nb, lp, d_mv), F32),
                   jax.ShapeDtypeStruct((nb, hm, MLSTM_QK_DIM, MLSTM_V_DIM), F32),
                   jax.ShapeDtypeStruct((nb, hm, 1, MLSTM_QK_DIM), F32),
                   jax.ShapeDtypeStruct((nb, hm, 1, LANES), F32)),
        scratch_shapes=[pltpu.VMEM((hm, MLSTM_QK_DIM, MLSTM_V_DIM), F32),
                        pltpu.VMEM((hm, 1, MLSTM_QK_DIM), F32),
                        pltpu.VMEM((hm, 1, LANES), F32)],
        compiler_params=_cparams("parallel", "arbitrary"),
        name="mlstm_prompt",
    )(qm, km, vm, gates, cum_l, gates_t, cum_lt)


def _paged_kernel(pt_ref, q_ref, kn_ref, vn_ref, lfn_ref, *refs, pps):
    k_refs = refs[:pps]
    v_refs = refs[pps:2 * pps]
    lf_refs = refs[2 * pps:3 * pps]
    o_ref, lf_sc, bias_sc, m_sc, l_sc, acc_sc, carry_sc = refs[3 * pps:]
    g = pl.program_id(1)
    nh, dh = N_ATT_HEADS, ATT_HEAD_DIM
    page = k_refs[0].shape[0]
    width = page * nh
    q8 = q_ref[0]

    @pl.when(g == 0)
    def _():
        m_sc[...] = jnp.sum(q8 * kn_ref[0], axis=-1, keepdims=True)
        l_sc[...] = jnp.ones(l_sc.shape, F32)
        acc_sc[...] = vn_ref[0]
        carry_sc[...] = jnp.zeros(carry_sc.shape, F32)

    for u in range(pps):
        lf_sc[u:u + 1, :] = lf_refs[u][...]
    lf = lf_sc[...]
    lane = lax.broadcasted_iota(I32, (pps, width), 1)
    incl = lf
    k = nh
    while k < width:
        incl = incl + jnp.where(lane < width - k, pltpu.roll(incl, width - k, axis=1), 0.0)
        k *= 2
    tot = jnp.where(lane < nh, incl, 0.0)
    k = nh
    while k < width:
        tot = tot + pltpu.roll(tot, k, axis=1)
        k *= 2
    run = carry_sc[...]
    for u in range(pps):
        bias_sc[u:u + 1, :] = run
        run = run + tot[u:u + 1, :]
    carry_sc[...] = run
    bias = bias_sc[...] + (incl - lf) + lfn_ref[0]

    own = lax.broadcasted_iota(I32, (nh, width), 1) % nh == lax.broadcasted_iota(I32, (nh, width), 0)
    q_bf = q8.astype(BF16)
    scores = []
    m_old = m_sc[...]
    m_new = m_old
    for u in range(pps):
        kf = k_refs[u][...].reshape(width, dh).astype(BF16)
        s = jnp.where(own, _dot_nt(q_bf, kf) + bias[u:u + 1, :], NEG)
        m_new = jnp.maximum(m_new, jnp.max(s, axis=-1, keepdims=True))
        scores.append(s)
    alpha = jnp.exp(m_old - m_new)
    l_new = alpha * l_sc[...]
    acc = alpha * acc_sc[...]
    for u in range(pps):
        p = jnp.exp(scores[u] - m_new)
        l_new = l_new + jnp.sum(p, axis=-1, keepdims=True)
        acc = acc + _dot(p.astype(BF16), v_refs[u][...].reshape(width, dh).astype(BF16))
    m_sc[...] = m_new
    l_sc[...] = l_new
    acc_sc[...] = acc

    @pl.when(g == pl.num_programs(1) - 1)
    def _():
        o_ref[0] = acc / l_new


def _paged_attention(q, k_new, v_new, lf_new_flat, cache_k, cache_v, cache_lf_flat, page_table):
    db, n_pages = page_table.shape
    _, n_pool, page, nh, dh = cache_k.shape
    width = page * nh
    pps = PAGES_PER_STEP
    steps = n_pages // pps
    tok = lambda: pl.BlockSpec((1, nh, dh), lambda b, g, pt: (b, 0, 0))

    def page_id(b, g, pt, u):
        return pt[b * n_pages + (n_pages - 1 - (g * pps + u))]

    kv_specs = [pl.BlockSpec((None, None, page, nh, dh),
                             functools.partial(lambda b, g, pt, u: (0, page_id(b, g, pt, u), 0, 0, 0), u=u))
                for u in range(pps)]
    lf_specs = [pl.BlockSpec((None, 1, width),
                             functools.partial(lambda b, g, pt, u: (page_id(b, g, pt, u), 0, 0), u=u))
                for u in range(pps)]
    grid_spec = pltpu.PrefetchScalarGridSpec(
        num_scalar_prefetch=1,
        grid=(db, steps),
        in_specs=[tok(), tok(), tok(), pl.BlockSpec((1, 1, width), lambda b, g, pt: (b, 0, 0))]
        + kv_specs + kv_specs + lf_specs,
        out_specs=tok(),
        scratch_shapes=[pltpu.VMEM((pps, width), F32), pltpu.VMEM((pps, width), F32),
                        pltpu.VMEM((nh, 1), F32), pltpu.VMEM((nh, 1), F32),
                        pltpu.VMEM((nh, dh), F32), pltpu.VMEM((1, width), F32)],
    )
    return pl.pallas_call(
        functools.partial(_paged_kernel, pps=pps),
        grid_spec=grid_spec,
        out_shape=jax.ShapeDtypeStruct((db, nh, dh), F32),
        compiler_params=_cparams("parallel", "arbitrary"),
        name="paged_fox",
    )(page_table.reshape(-1), q, k_new, v_new, lf_new_flat, *([cache_k] * pps), *([cache_v] * pps),
      *([cache_lf_flat] * pps))


def _mstep_kernel(q_ref, k_ref, v_ref, ig_ref, lf_ref, c_ref, n_ref, m_ref,
                  h_ref, c_out_ref, n_out_ref, m_out_ref):
    for hh in range(N_MLSTM_HEADS):
        q = q_ref[0, hh].astype(F32)
        k = k_ref[0, hh].astype(F32)
        v = v_ref[0, hh].astype(F32)
        ig = ig_ref[0, hh]
        lf = lf_ref[0, hh]
        c0 = c_ref[0, hh]
        n0 = n_ref[0, hh]
        m0 = m_ref[0, hh]
        inter = lf + m0
        m = jnp.maximum(inter, ig)
        w_intra = jnp.exp(ig - m)
        w_inter = jnp.exp(inter - m)
        s = jnp.sum(q * k, axis=0, keepdims=True) * w_intra
        num = s * v + w_inter * jnp.sum(q * c0, axis=0, keepdims=True)
        den = s + w_inter * jnp.sum(q * n0, axis=0, keepdims=True)
        h_ref[0, hh] = num / jnp.maximum(jnp.abs(den), jnp.exp(-m))
        c_out_ref[0, hh] = w_inter * c0 + w_intra * (k * v)
        n_out_ref[0, hh] = w_inter * n0 + w_intra * k
        m_out_ref[0, hh] = m


def _mlstm_step(q_col, k_col, v_row, ig, lf, state_c, state_n_col, state_m):
    db, hm, dk, dv = state_c.shape
    spec = lambda a, b_: pl.BlockSpec((1, hm, a, b_), lambda i: (i, 0, 0, 0))
    return pl.pallas_call(
        _mstep_kernel,
        grid=(db,),
        in_specs=[spec(dk, 1), spec(dk, 1), spec(1, dv), spec(1, 1), spec(1, 1), spec(dk, dv), spec(dk, 1), spec(1, 1)],
        out_specs=(spec(1, dv), spec(dk, dv), spec(dk, 1), spec(1, 1)),
        out_shape=(jax.ShapeDtypeStruct((db, hm, 1, dv), F32), jax.ShapeDtypeStruct((db, hm, dk, dv), F32),
                   jax.ShapeDtypeStruct((db, hm, dk, 1), F32), jax.ShapeDtypeStruct((db, hm, 1, 1), F32)),
        compiler_params=_cparams("parallel"),
        name="mlstm_step",
    )(q_col, k_col, v_row, ig, lf, state_c, state_n_col, state_m)


def _mix_kernel(att_ref, hm_ref, og_ref, h_ref, gm_ref, w_ref, lg_ref, lb_ref, wrh_ref, wrl_ref, br_ref,
                h1_ref, h1p_ref, e_ref, gate_ref, *, alpha):
    d_att = att_ref.shape[-1]
    parts = []
    for hh in range(N_MLSTM_HEADS):
        vv = slice(hh * MLSTM_V_DIM, (hh + 1) * MLSTM_V_DIM)
        x = hm_ref[:, vv]
        xn = x * lax.rsqrt(jnp.mean(x * x, axis=-1, keepdims=True) + RMS_EPS) * gm_ref[:, vv]
        parts.append((og_ref[:, vv] * xn).astype(BF16))
    mix = _dot(att_ref[...].astype(BF16), w_ref[0:d_att, :])
    for hh in range(N_MLSTM_HEADS):
        r0 = d_att + hh * MLSTM_V_DIM
        mix = mix + _dot(parts[hh], w_ref[r0:r0 + MLSTM_V_DIM, :])
    h1 = _layer_norm(alpha * h_ref[...] + mix, lg_ref[...], lb_ref[...])
    h1_ref[...] = h1
    half = h1.shape[-1] // 2
    hi = lax.bitcast_convert_type(h1[:, :half].astype(BF16).astype(F32), jnp.uint32)
    lo = lax.bitcast_convert_type(h1[:, half:].astype(BF16).astype(F32), jnp.uint32)
    h1p_ref[...] = hi | (lo >> 16)

    logits = _dot3(h1, wrh_ref[...], wrl_ref[...]) + br_ref[...]
    lane = lax.broadcasted_iota(I32, logits.shape, 1).astype(F32)
    e_out = jnp.zeros(logits.shape, F32)
    top = []
    for kk in range(TOP_K):
        mx = jnp.max(logits, axis=-1, keepdims=True)
        idx = jnp.min(jnp.where(logits == mx, lane, float(LANES)), axis=-1, keepdims=True)
        logits = jnp.where(lane == idx, -jnp.inf, logits)
        e_out = jnp.where(lane == float(kk), idx, e_out)
        top.append(mx)
    ex = [jnp.exp(t - top[0]) for t in top]
    inv = 1.0 / (ex[0] + ex[1] + ex[2] + ex[3])
    gate = jnp.zeros(logits.shape, F32)
    for kk in range(TOP_K):
        gate = jnp.where(lane == float(kk), ex[kk] * inv, gate)
    e_ref[...] = e_out.astype(I32)
    gate_ref[...] = gate


def _mix(att, hm, og, h, prm, tm, alpha):
    rows, d = h.shape
    d_att = att.shape[-1]
    d_mv = hm.shape[-1]
    row = lambda w: pl.BlockSpec((tm, w), lambda i: (i, 0))
    const = lambda r, c: pl.BlockSpec((r, c), lambda i: (0, 0), pipeline_mode=pl.Buffered(1))
    return pl.pallas_call(
        functools.partial(_mix_kernel, alpha=alpha),
        grid=(rows // tm,),
        in_specs=[row(d_att), row(d_mv), row(d_mv), row(d), const(1, d_mv), const(d_att + d_mv, d),
                  const(1, d), const(1, d), const(d, LANES), const(d, LANES), const(1, LANES)],
        out_specs=(row(d), row(d // 2), row(LANES), row(LANES)),
        out_shape=(jax.ShapeDtypeStruct((rows, d), F32), jax.ShapeDtypeStruct((rows, d // 2), jnp.uint32),
                   jax.ShapeDtypeStruct((rows, LANES), I32), jax.ShapeDtypeStruct((rows, LANES), F32)),
        compiler_params=_cparams("parallel"),
        name="mix",
    )(att, hm, og, h, prm["g_m_out"], prm["w_out"], prm["ln1_g"], prm["ln1_b"], prm["wr_hi"], prm["wr_lo"],
      prm["b_router"])


def _moe_kernel(ie_ref, ns_ref, iv_ref, st_ref, sp_ref,
                x_hbm, wg_ref, wu_ref, bg_ref, bu_ref, wd_ref, bd_ref,
                y_hbm, xbuf, xb, acc, gsem, ssem, *, n_tok, n_items, n_ft):
    i = pl.program_id(0)
    f = pl.program_id(1)
    nf = pl.num_programs(1)
    slot = i % 2
    other = 1 - slot
    n_sub = ns_ref[i]
    valid = iv_ref[i] == 1
    max_sub = MOE_ROWS // MOE_SUB
    half = xbuf.shape[-1]
    dump_base = TOP_K * n_tok

    def gather_rows(item, sl, row0, count):
        base = st_ref[item] + row0
        dst = xbuf.at[sl, pl.ds(row0, count), :]
        for r in range(count):
            pair = sp_ref[base + r]
            tok = jnp.maximum(pair, 0) >> LOG2_TOP_K
            pltpu.make_async_copy(x_hbm.at[pl.ds(tok, 1), :], dst.at[pl.ds(r, 1), :], gsem.at[sl]).start()

    def scatter_rows(item, sl, row0, count):
        base = st_ref[item] + row0
        n_real = ns_ref[item] * MOE_SUB - row0
        dump = dump_base + sl * MOE_ROWS + row0
        src = acc.at[sl, pl.ds(row0, count), :]
        for r in range(count):
            pair = sp_ref[base + r]
            real = jnp.logical_and(pair >= 0, r < n_real)
            dst = jnp.where(real, (pair & (TOP_K - 1)) * n_tok + (pair >> LOG2_TOP_K), dump + r)
            pltpu.make_async_copy(src.at[pl.ds(r, 1), :], y_hbm.at[pl.ds(dst, 1), :], ssem.at[sl]).start()

    def gather_wait(sl):
        pltpu.make_async_copy(x_hbm.at[pl.ds(0, MOE_ROWS), :], xbuf.at[sl], gsem.at[sl]).wait()

    def scatter_wait(sl):
        pltpu.make_async_copy(acc.at[sl], y_hbm.at[pl.ds(0, MOE_ROWS), :], ssem.at[sl]).wait()

    def all_steps(fn):
        def body(step, carry):
            fn(pl.multiple_of(step * MOE_SUB, MOE_SUB))
            return carry
        lax.fori_loop(0, MOE_ROWS // MOE_SUB, body, 0)

    nxt = jnp.minimum(i + 1, n_items - 1)
    has_next = jnp.logical_and(i + 1 < n_items, iv_ref[nxt] == 1)
    prv = jnp.maximum(i - 1, 0)
    has_prev = jnp.logical_and(i >= 1, iv_ref[prv] == 1)
    prv2 = jnp.maximum(i - 2, 0)
    has_prev2 = jnp.logical_and(i >= 2, iv_ref[prv2] == 1)

    @pl.when(f == 0)
    def _():
        @pl.when(jnp.logical_and(i == 0, valid))
        def _():
            all_steps(lambda row0: gather_rows(0, 0, row0, MOE_SUB))

        @pl.when(valid)
        def _():
            gather_wait(slot)
            packed = xbuf[slot]
            hi = lax.bitcast_convert_type(packed & jnp.uint32(0xFFFF0000), F32)
            lo = lax.bitcast_convert_type(packed << 16, F32)
            xb[:, 0:half] = hi.astype(BF16)
            xb[:, half:2 * half] = lo.astype(BF16)

        @pl.when(has_prev2)
        def _():
            scatter_wait(slot)

        @pl.when(valid)
        def _():
            acc[slot] = jnp.broadcast_to(bd_ref[0], acc.shape[1:])

    per_step = MOE_ROWS // n_ft
    row0 = pl.multiple_of(f * per_step, per_step)

    @pl.when(has_next)
    def _():
        gather_rows(i + 1, other, row0, per_step)

    @pl.when(has_prev)
    def _():
        scatter_rows(i - 1, other, row0, per_step)

    for nn in range(1, max_sub + 1):
        @pl.when(jnp.logical_and(valid, n_sub == nn))
        def _():
            rows = slice(0, nn * MOE_SUB)
            xs = xb[rows, :]
            gp = _dot(xs, wg_ref[0].astype(BF16)) + bg_ref[0]
            up = _dot(xs, wu_ref[0].astype(BF16)) + bu_ref[0]
            gp = jnp.minimum(gp, SWIGLU_LIMIT)
            up = jnp.clip(up, -SWIGLU_LIMIT, SWIGLU_LIMIT)
            act = gp * _sigmoid(SWIGLU_ALPHA * gp) * (up + 1.0)
            acc[slot, rows, :] += _dot(act.astype(BF16), wd_ref[0].astype(BF16))

    @pl.when(jnp.logical_and(i == n_items - 1, f == nf - 1))
    def _():
        @pl.when(valid)
        def _():
            all_steps(lambda row0: scatter_rows(i, slot, row0, MOE_SUB))
            scatter_wait(slot)

        @pl.when(has_prev)
        def _():
            scatter_wait(other)


def _moe(x_packed, item_e, item_ns, item_valid, item_start, slot_pair, w_gate_up, b_gate_up, w_down, b_down):
    n_tok, half = x_packed.shape
    d = 2 * half
    n_exp, _, two_f = w_gate_up.shape
    nf = two_f // 2 // MOE_FT
    n_items = item_e.shape[0]
    assert MOE_ROWS % nf == 0

    def ftile(i, f, iv):
        return f * iv[i] + (nf - 1) * (1 - iv[i])

    grid_spec = pltpu.PrefetchScalarGridSpec(
        num_scalar_prefetch=5,
        grid=(n_items, nf),
        in_specs=[
            pl.BlockSpec(memory_space=pl.ANY),
            pl.BlockSpec((1, d, MOE_FT), lambda i, f, ie, ns, iv, st, sp: (ie[i], 0, ftile(i, f, iv))),
            pl.BlockSpec((1, d, MOE_FT), lambda i, f, ie, ns, iv, st, sp: (ie[i], 0, nf + ftile(i, f, iv))),
            pl.BlockSpec((1, 1, MOE_FT), lambda i, f, ie, ns, iv, st, sp: (ie[i], 0, ftile(i, f, iv))),
            pl.BlockSpec((1, 1, MOE_FT), lambda i, f, ie, ns, iv, st, sp: (ie[i], 0, nf + ftile(i, f, iv))),
            pl.BlockSpec((1, MOE_FT, d), lambda i, f, ie, ns, iv, st, sp: (ie[i], ftile(i, f, iv), 0)),
            pl.BlockSpec((1, 1, d), lambda i, f, ie, ns, iv, st, sp: (ie[i], 0, 0)),
        ],
        out_specs=pl.BlockSpec(memory_space=pl.ANY),
        scratch_shapes=[pltpu.VMEM((2, MOE_ROWS, half), jnp.uint32), pltpu.VMEM((MOE_ROWS, d), BF16),
                        pltpu.VMEM((2, MOE_ROWS, d), F32),
                        pltpu.SemaphoreType.DMA((2,)), pltpu.SemaphoreType.DMA((2,))],
    )
    return pl.pallas_call(
        functools.partial(_moe_kernel, n_tok=n_tok, n_items=n_items, n_ft=nf),
        grid_spec=grid_spec,
        out_shape=jax.ShapeDtypeStruct((TOP_K * n_tok + 2 * MOE_ROWS, d), F32),
        compiler_params=_cparams("arbitrary", "arbitrary"),
        name="moe",
    )(item_e, item_ns, item_valid, item_start, slot_pair, x_packed, w_gate_up, w_gate_up,
      b_gate_up.reshape(n_exp, 1, two_f), b_gate_up.reshape(n_exp, 1, two_f), w_down, b_down.reshape(n_exp, 1, d))


def _combine_kernel(y0_ref, y1_ref, y2_ref, y3_ref, gate_ref, h1_ref, lg_ref, lb_ref, o_ref, *, alpha):
    acc = alpha * h1_ref[...]
    for kk, y_ref in enumerate((y0_ref, y1_ref, y2_ref, y3_ref)):
        acc = acc + y_ref[...] * gate_ref[:, kk:kk + 1]
    o_ref[...] = _layer_norm(acc, lg_ref[...], lb_ref[...])


def _combine(y_pairs, gate, h1, ln_g, ln_b, tm, alpha):
    rows, d = h1.shape
    nblk = rows // tm
    const = lambda r, c: pl.BlockSpec((r, c), lambda i: (0, 0))
    y_specs = [pl.BlockSpec((tm, d), functools.partial(lambda i, kk: (kk * nblk + i, 0), kk=kk)) for kk in range(TOP_K)]
    return pl.pallas_call(
        functools.partial(_combine_kernel, alpha=alpha),
        grid=(nblk,),
        in_specs=y_specs + [pl.BlockSpec((tm, LANES), lambda i: (i, 0)),
                            pl.BlockSpec((tm, d), lambda i: (i, 0)), const(1, d), const(1, d)],
        out_specs=pl.BlockSpec((tm, d), lambda i: (i, 0)),
        out_shape=jax.ShapeDtypeStruct((rows, d), F32),
        compiler_params=_cparams("parallel"),
        name="combine",
    )(*([y_pairs] * TOP_K), gate, h1, ln_g, ln_b)


def _route(top_e):
    n_pairs = top_e.shape[0] * TOP_K
    per_item = MOE_ROWS // MOE_SUB
    pair_e = top_e.reshape(-1)
    order = jnp.argsort(pair_e, stable=True).astype(I32)
    e_sorted = pair_e[order]
    bounds = jnp.searchsorted(e_sorted, jnp.arange(N_EXPERTS + 1, dtype=I32)).astype(I32)
    start = bounds[:-1]
    counts = bounds[1:] - start
    n_sb = (counts + MOE_SUB - 1) // MOE_SUB
    sb_end = jnp.cumsum(n_sb)
    sb_start = sb_end - n_sb
    max_sb = n_pairs // MOE_SUB + N_EXPERTS
    slot = jnp.arange(max_sb * MOE_SUB, dtype=I32)
    e_slot = jnp.minimum(jnp.searchsorted(sb_end * MOE_SUB, slot, side="right"), N_EXPERTS - 1)
    rank = slot - sb_start[e_slot] * MOE_SUB
    src = jnp.clip(start[e_slot] + rank, 0, n_pairs - 1)
    slot_pair = jnp.where(rank < counts[e_slot], order[src], -1).astype(I32)
    slot_pair = jnp.concatenate([slot_pair, jnp.full((MOE_ROWS,), -1, I32)])

    n_it = (n_sb + per_item - 1) // per_item
    it_end = jnp.cumsum(n_it)
    it_start = it_end - n_it
    max_items = (max_sb + (per_item - 1) * N_EXPERTS) // per_item
    idx = jnp.arange(max_items, dtype=I32)
    n_valid = it_end[-1]
    item_valid = (idx < n_valid).astype(I32)
    idc = jnp.minimum(idx, n_valid - 1)
    item_e = jnp.minimum(jnp.searchsorted(it_end, idc, side="right"), N_EXPERTS - 1).astype(I32)
    chunk = idc - it_start[item_e]
    item_start = ((sb_start[item_e] + per_item * chunk) * MOE_SUB).astype(I32)
    item_ns = (jnp.clip(n_sb[item_e] - per_item * chunk, 0, per_item) * item_valid).astype(I32)
    return item_e, item_ns, item_valid, item_start, slot_pair


def _row_tile(n_rows, cap=256):
    return max(t for t in range(8, cap + 1, 8) if n_rows % t == 0)


def _prep_params(ln_in_g, ln_in_b, w_in, b_att_f, g_att_q, g_att_k, b_m_i, b_m_f, g_m_out, w_out,
                 ln1_g, ln1_b, w_router, b_router, ln2_g, ln2_b):
    d = w_in.shape[0]
    d_att = N_ATT_HEADS * ATT_HEAD_DIM
    d_mqk = N_MLSTM_HEADS * MLSTM_QK_DIM
    d_mv = N_MLSTM_HEADS * MLSTM_V_DIM
    sizes = (d_att, d_att, d_att, N_ATT_HEADS, d_mqk, d_mqk, d_mv, N_MLSTM_HEADS, N_MLSTM_HEADS, d_mv)
    offs = [0]
    for s in sizes:
        offs.append(offs[-1] + s)
    cols = lambda i: w_in[:, offs[i]:offs[i + 1]]
    w_main = jnp.concatenate([cols(0), cols(1), cols(2), cols(4), cols(5), cols(6), cols(9)], axis=1).astype(BF16)
    w_gate = jnp.concatenate([cols(3), cols(7), cols(8), jnp.zeros((d, LANES - N_GATES), F32)], axis=1)
    wg_hi, wg_lo = _split2(w_gate)
    gate_bias = jnp.concatenate([b_att_f, b_m_i, b_m_f, jnp.zeros((LANES - N_GATES,), F32)]).reshape(1, LANES)
    w_r = jnp.concatenate([w_router, jnp.zeros((d, LANES - N_EXPERTS), F32)], axis=1)
    wr_hi, wr_lo = _split2(w_r)
    b_r = jnp.concatenate([b_router, jnp.full((LANES - N_EXPERTS,), NEG, F32)]).reshape(1, LANES)
    return dict(
        ln_in_g=ln_in_g.reshape(1, d), ln_in_b=ln_in_b.reshape(1, d), w_main=w_main, wg_hi=wg_hi, wg_lo=wg_lo,
        gate_bias=gate_bias, g_att_q=g_att_q.reshape(1, -1), g_att_k=g_att_k.reshape(1, -1),
        g_m_out=g_m_out.reshape(1, -1), w_out=w_out.astype(BF16), ln1_g=ln1_g.reshape(1, d),
        ln1_b=ln1_b.reshape(1, d), wr_hi=wr_hi, wr_lo=wr_lo, b_router=b_r,
        ln2_g=ln2_g.reshape(1, d), ln2_b=ln2_b.reshape(1, d))


def kernel(x_prompt, x_sample, cache_k, cache_v, cache_logf, state_C, state_n, state_m, page_table,
           meta_tokens, ln_in_g, ln_in_b, w_in, b_att_f, g_att_q, g_att_k, b_m_i, b_m_f, g_m_out, w_out,
           ln1_g, ln1_b, w_router, b_router, w_gate_up, b_gate_up, w_down, b_down, ln2_g, ln2_b):
    depth = w_in.shape[0]
    assert depth == 1, "single-layer trunk only"
    nb, seq, d = x_prompt.shape
    db = x_sample.shape[0]
    assert x_sample.shape[1] == 1, "one new token per sample"
    alpha = (2.0 * depth) ** 0.25
    d_att = N_ATT_HEADS * ATT_HEAD_DIM
    hm = N_MLSTM_HEADS
    prm = _prep_params(ln_in_g, ln_in_b, w_in[0], b_att_f[0], g_att_q[0], g_att_k[0], b_m_i[0], b_m_f[0],
                       g_m_out[0], w_out[0], ln1_g[0], ln1_b[0], w_router[0], b_router[0], ln2_g[0], ln2_b[0])

    length = N_META + seq
    lp = pl.cdiv(length, CHUNK) * CHUNK
    meta = jnp.broadcast_to(meta_tokens[None], (nb, N_META, d))
    xp = jnp.concatenate([meta, x_prompt, jnp.zeros((nb, lp - length, d), F32)], axis=1)
    (hp, qa, k32, kb, v32, vb, gates, qm, km, vm, og) = _proj(xp, length, lp // 8, prm)
    cum_l, cum_lt, gates_t, k_aug, q_aug = _gate_scan(gates)
    att_p = _fox_prompt(qa, q_aug, kb, k_aug, vb)
    hm_p, c_p, n_p, m_p = _mlstm_prompt(qm, km, vm, gates, cum_l, gates_t, cum_lt)
    rows_p = nb * lp
    h1_p, h1p_p, e_p, gate_p = _mix(att_p.reshape(rows_p, d_att), hm_p.reshape(rows_p, -1), og.reshape(rows_p, -1),
                                    hp.reshape(rows_p, d), prm, 256, alpha)

    xs = x_sample.reshape(1, db, d)
    (hs, qa_s, k32_s, _, v32_s, _, gates_s, qm_s, km_s, vm_s, og_s) = _proj(xs, db, db, prm)
    n_pool, page = cache_k.shape[1], cache_k.shape[2]
    nh, dh = N_ATT_HEADS, ATT_HEAD_DIM
    lf_s = gates_s[0, :, GATE_LF_ATT:GATE_LF_ATT + nh]
    att_s = _paged_attention(
        qa_s.astype(F32).reshape(db, nh, dh), k32_s.reshape(db, nh, dh), v32_s.reshape(db, nh, dh),
        jnp.tile(lf_s, (1, page)).reshape(db, 1, page * nh), cache_k, cache_v,
        cache_logf[0].reshape(n_pool, 1, page * nh), page_table)
    ig_s = gates_s[0, :, GATE_IG:GATE_IG + hm].reshape(db, hm, 1, 1)
    lfm_s = gates_s[0, :, GATE_LF_M:GATE_LF_M + hm].reshape(db, hm, 1, 1)
    hm_s, c_s, n_s, m_s = _mlstm_step(
        qm_s.astype(F32).reshape(db, hm, MLSTM_QK_DIM, 1), km_s.astype(F32).reshape(db, hm, MLSTM_QK_DIM, 1),
        vm_s.astype(F32).reshape(db, hm, 1, MLSTM_V_DIM), ig_s, lfm_s, state_C[0],
        state_n[0].reshape(db, hm, MLSTM_QK_DIM, 1), state_m[0].reshape(db, hm, 1, 1))
    h1_s, h1p_s, e_s, gate_s = _mix(att_s.reshape(db, d_att), hm_s.reshape(db, -1), og_s.reshape(db, -1),
                                    hs.reshape(db, d), prm, db, alpha)

    h1 = jnp.concatenate([h1_p, h1_s], axis=0)
    x_packed = jnp.concatenate([h1p_p, h1p_s], axis=0)
    top_e = jnp.concatenate([e_p, e_s], axis=0)[:, :TOP_K]
    gate = jnp.concatenate([gate_p, gate_s], axis=0)
    item_e, item_ns, item_valid, item_start, slot_pair = _route(top_e)
    y_pairs = _moe(x_packed, item_e, item_ns, item_valid, item_start, slot_pair,
                   w_gate_up[0], b_gate_up[0], w_down[0], b_down[0])
    out = _combine(y_pairs, gate, h1, prm["ln2_g"], prm["ln2_b"], _row_tile(rows_p + db), alpha)

    y_prompt = out[:rows_p].reshape(nb, lp, d)[:, N_META:length]
    y_sample = out[rows_p:].reshape(db, 1, d)
    k_prompt = k32.reshape(1, nb, length, N_ATT_HEADS, ATT_HEAD_DIM)
    v_prompt = v32.reshape(1, nb, length, N_ATT_HEADS, ATT_HEAD_DIM)
    lf_prompt = gates[:, :length, GATE_LF_ATT:GATE_LF_ATT + N_ATT_HEADS][None]
    return (y_prompt, y_sample, k_prompt, v_prompt, lf_prompt,
            c_p[None], n_p.reshape(1, nb, hm, MLSTM_QK_DIM), m_p[:, :, 0, 0][None],
            k32_s.reshape(1, db, 1, N_ATT_HEADS, ATT_HEAD_DIM), v32_s.reshape(1, db, 1, N_ATT_HEADS, ATT_HEAD_DIM),
            lf_s.reshape(1, db, 1, nh),
            c_s[None], n_s.reshape(1, db, hm, MLSTM_QK_DIM), m_s.reshape(1, db, hm))
```

```python
import functools

import jax
import jax.numpy as jnp
from jax import lax
from jax.experimental import pallas as pl
from jax.experimental.pallas import tpu as pltpu

F32 = jnp.float32
BF16 = jnp.bfloat16
I32 = jnp.int32

N_META = 16
CHUNK = 128
ATT_HEAD_DIM = 128
N_ATT_HEADS = 8
N_MLSTM_HEADS = 4
MLSTM_QK_DIM = 128
MLSTM_V_DIM = 256
N_EXPERTS = 32
TOP_K = 4
LOG2_TOP_K = 2
SWIGLU_ALPHA = 1.702
SWIGLU_LIMIT = 7.0
LN_EPS = 1e-5
RMS_EPS = 1e-6
NEG = -1e30

LANES = 128
GATE_LF_ATT = 0
GATE_IG = N_ATT_HEADS
GATE_LF_M = GATE_IG + N_MLSTM_HEADS
N_GATES = GATE_LF_M + N_MLSTM_HEADS

VMEM_LIMIT = 56 * 1024 * 1024

MOE_ROWS = 1152
MOE_SUB = 128
MOE_FT = 256
MOE_BLOCK = 512
PAGES_PER_STEP = 8
FOX_TQ = 256


def _cparams(*sem):
    return pltpu.CompilerParams(dimension_semantics=sem, vmem_limit_bytes=VMEM_LIMIT)


def _split2(x):
    hi = x.astype(BF16)
    lo = (x - hi.astype(F32)).astype(BF16)
    return hi, lo


def _split3(x):
    hi = x.astype(BF16)
    r1 = x - hi.astype(F32)
    mid = r1.astype(BF16)
    lo = (r1 - mid.astype(F32)).astype(BF16)
    return hi, mid, lo


def _dot(a, b):
    return jnp.dot(a, b, preferred_element_type=F32)


def _dot_nt(a, b):
    return lax.dot_general(a, b, (((1,), (1,)), ((), ())), preferred_element_type=F32)


def _dot_tn(a, b):
    return lax.dot_general(a, b, (((0,), (0,)), ((), ())), preferred_element_type=F32)


def _dot3(x_f32, w_hi, w_lo):
    x_hi, x_lo = _split2(x_f32)
    return _dot(x_hi, w_hi) + (_dot(x_lo, w_hi) + _dot(x_hi, w_lo))


def _layer_norm(x, g, b):
    mu = jnp.mean(x, axis=-1, keepdims=True)
    xc = x - mu
    var = jnp.mean(xc * xc, axis=-1, keepdims=True)
    return xc * lax.rsqrt(var + LN_EPS) * g + b


def _log_sigmoid(z):
    return jnp.minimum(z, 0.0) - jnp.log(1.0 + jnp.exp(-jnp.abs(z)))


def _sigmoid(z):
    return 1.0 / (1.0 + jnp.exp(-z))


def _proj_kernel(x_ref, lng_ref, lnb_ref, w_ref, wgh_ref, wgl_ref, gb_ref, gq_ref, gk_ref,
                 h_ref, q_ref, k32_ref, kb_ref, v32_ref, vb_ref, g_ref, qm_ref, km_ref, vm_ref, og_ref,
                 *, tm, valid_len):
    d_att = N_ATT_HEADS * ATT_HEAD_DIM
    d_mqk = N_MLSTM_HEADS * MLSTM_QK_DIM
    d_mv = N_MLSTM_HEADS * MLSTM_V_DIM
    h = _layer_norm(x_ref[0], lng_ref[...], lnb_ref[...])
    h_ref[0] = h
    hb = h.astype(BF16)

    def seg(c0, width):
        return _dot(hb, w_ref[:, c0:c0 + width])

    scale = ATT_HEAD_DIM ** -0.5
    qa = seg(0, d_att)
    ka = seg(d_att, d_att)
    for hh in range(N_ATT_HEADS):
        sl = slice(hh * ATT_HEAD_DIM, (hh + 1) * ATT_HEAD_DIM)
        qh = qa[:, sl]
        qn = qh * lax.rsqrt(jnp.mean(qh * qh, axis=-1, keepdims=True) + RMS_EPS) * gq_ref[...]
        q_ref[0, :, sl] = (qn * scale).astype(BF16)
        kh = ka[:, sl]
        kn = kh * lax.rsqrt(jnp.mean(kh * kh, axis=-1, keepdims=True) + RMS_EPS) * gk_ref[...]
        k32_ref[0, :, sl] = kn
        kb_ref[0, :, sl] = kn.astype(BF16)
    va = seg(2 * d_att, d_att)
    v32_ref[0] = va
    vb_ref[0] = va.astype(BF16)
    c0 = 3 * d_att
    qm_ref[0] = seg(c0, d_mqk).astype(BF16)
    km_ref[0] = (seg(c0 + d_mqk, d_mqk) * (MLSTM_QK_DIM ** -0.5)).astype(BF16)
    vm_ref[0] = seg(c0 + 2 * d_mqk, d_mv).astype(BF16)
    og_ref[0] = _sigmoid(seg(c0 + 2 * d_mqk + d_mv, d_mv))

    z = _dot3(h, wgh_ref[...], wgl_ref[...]) + gb_ref[...]
    lane = lax.broadcasted_iota(I32, z.shape, 1)
    row = pl.program_id(1) * tm + lax.broadcasted_iota(I32, z.shape, 0)
    is_ig = jnp.logical_and(lane >= GATE_IG, lane < GATE_LF_M)
    gates = jnp.where(is_ig, z, jnp.where(lane < N_GATES, _log_sigmoid(z), 0.0))
    gates = jnp.where(row < valid_len, gates, jnp.where(is_ig, NEG, 0.0))
    g_ref[0] = gates


def _proj(x, valid_len, tm, prm):
    nb, lp, d = x.shape
    d_att = N_ATT_HEADS * ATT_HEAD_DIM
    d_mqk = N_MLSTM_HEADS * MLSTM_QK_DIM
    d_mv = N_MLSTM_HEADS * MLSTM_V_DIM
    n_main = prm["w_main"].shape[1]
    grid = (nb, lp // tm)
    row = lambda w: pl.BlockSpec((1, tm, w), lambda b, i: (b, i, 0))
    const = lambda r, c: pl.BlockSpec((r, c), lambda b, i: (0, 0), pipeline_mode=pl.Buffered(1))
    out_shape = (
        jax.ShapeDtypeStruct((nb, lp, d), F32),
        jax.ShapeDtypeStruct((nb, lp, d_att), BF16),
        jax.ShapeDtypeStruct((nb, valid_len, d_att), F32),
        jax.ShapeDtypeStruct((nb, lp, d_att), BF16),
        jax.ShapeDtypeStruct((nb, valid_len, d_att), F32),
        jax.ShapeDtypeStruct((nb, lp, d_att), BF16),
        jax.ShapeDtypeStruct((nb, lp, LANES), F32),
        jax.ShapeDtypeStruct((nb, lp, d_mqk), BF16),
        jax.ShapeDtypeStruct((nb, lp, d_mqk), BF16),
        jax.ShapeDtypeStruct((nb, lp, d_mv), BF16),
        jax.ShapeDtypeStruct((nb, lp, d_mv), F32),
    )
    out_specs = (row(d), row(d_att), row(d_att), row(d_att), row(d_att), row(d_att), row(LANES),
                 row(d_mqk), row(d_mqk), row(d_mv), row(d_mv))
    return pl.pallas_call(
        functools.partial(_proj_kernel, tm=tm, valid_len=valid_len),
        grid=grid,
        in_specs=[row(d), const(1, d), const(1, d), const(d, n_main), const(d, LANES), const(d, LANES),
                  const(1, LANES), const(1, ATT_HEAD_DIM), const(1, ATT_HEAD_DIM)],
        out_specs=out_specs,
        out_shape=out_shape,
        compiler_params=_cparams("parallel", "arbitrary"),
        name="proj",
    )(x, prm["ln_in_g"], prm["ln_in_b"], prm["w_main"], prm["wg_hi"], prm["wg_lo"], prm["gate_bias"],
      prm["g_att_q"], prm["g_att_k"])


def _gate_scan_kernel(g_ref, cl_ref, clt_ref, gt_ref, ka_ref, qa_ref, *, n_chunks):
    r = lax.broadcasted_iota(I32, (CHUNK, CHUNK), 0)
    c = lax.broadcasted_iota(I32, (CHUNK, CHUNK), 1)
    tri = jnp.where(c <= r, 1.0, 0.0).astype(BF16)
    carry = jnp.zeros((1, LANES), F32)
    for ci in range(n_chunks):
        sl = slice(ci * CHUNK, (ci + 1) * CHUNK)
        g = g_ref[0, sl, :]
        hi, mid, lo = _split3(g)
        loc = _dot(tri, hi) + (_dot(tri, mid) + _dot(tri, lo))
        glob = loc + carry
        carry = glob[CHUNK - 1:CHUNK, :]
        cl_ref[0, sl, :] = loc
        clt_ref[0, ci] = loc.T
        gt_ref[0, ci] = g.T
        for hh in range(N_ATT_HEADS):
            cum = glob[:, GATE_LF_ATT + hh:GATE_LF_ATT + hh + 1]
            t0 = cum.astype(BF16).astype(F32)
            r1 = cum - t0
            t1 = r1.astype(BF16).astype(F32)
            t2 = (r1 - t1).astype(BF16).astype(F32)
            hs = slice(hh * ATT_HEAD_DIM, (hh + 1) * ATT_HEAD_DIM)
            ones = jnp.where(c < 3, 1.0, 0.0)
            terms = jnp.where(c == 3, t0, jnp.where(c == 4, t1, jnp.where(c == 5, t2, 0.0)))
            qa_ref[0, sl, hs] = (ones + terms).astype(BF16)
            ones_k = jnp.where(jnp.logical_and(c >= 3, c < 6), 1.0, 0.0)
            terms_k = jnp.where(c == 0, t0, jnp.where(c == 1, t1, jnp.where(c == 2, t2, 0.0)))
            ka_ref[0, sl, hs] = (ones_k - terms_k).astype(BF16)


def _gate_scan(gates):
    nb, lp, _ = gates.shape
    nc = lp // CHUNK
    d_att = N_ATT_HEADS * ATT_HEAD_DIM
    col = pl.BlockSpec((1, lp, LANES), lambda b: (b, 0, 0))
    rowt = pl.BlockSpec((1, nc, LANES, CHUNK), lambda b: (b, 0, 0, 0))
    aug = pl.BlockSpec((1, lp, d_att), lambda b: (b, 0, 0))
    return pl.pallas_call(
        functools.partial(_gate_scan_kernel, n_chunks=nc),
        grid=(nb,),
        in_specs=[col],
        out_specs=(col, rowt, rowt, aug, aug),
        out_shape=(jax.ShapeDtypeStruct((nb, lp, LANES), F32),
                   jax.ShapeDtypeStruct((nb, nc, LANES, CHUNK), F32),
                   jax.ShapeDtypeStruct((nb, nc, LANES, CHUNK), F32),
                   jax.ShapeDtypeStruct((nb, lp, d_att), BF16),
                   jax.ShapeDtypeStruct((nb, lp, d_att), BF16)),
        compiler_params=_cparams("parallel"),
        name="gate_scan",
    )(gates)


def _fox_kernel(q_ref, qa_ref, k_ref, ka_ref, v_ref, o_ref, kx_sc, vt_sc, qx_sc, m_sc, l_sc, acc_sc,
                *, n_chunks, tq):
    ti = pl.program_id(1)
    dh = ATT_HEAD_DIM

    @pl.when(ti == 0)
    def _():
        for j in range(n_chunks):
            rows = slice(j * CHUNK, (j + 1) * CHUNK)
            for hh in range(N_ATT_HEADS):
                sl = slice(hh * dh, (hh + 1) * dh)
                kx_sc[j, :, 2 * hh * dh:(2 * hh + 1) * dh] = k_ref[0, rows, sl]
                kx_sc[j, :, (2 * hh + 1) * dh:(2 * hh + 2) * dh] = ka_ref[0, rows, sl]
                vt_sc[hh, j] = v_ref[0, rows, sl].astype(F32).T.astype(BF16)

    for hh in range(N_ATT_HEADS):
        sl = slice(hh * dh, (hh + 1) * dh)
        qx_sc[:, 2 * hh * dh:(2 * hh + 1) * dh] = q_ref[0, :, sl]
        qx_sc[:, (2 * hh + 1) * dh:(2 * hh + 2) * dh] = qa_ref[0, :, sl]
    m_sc[...] = jnp.full(m_sc.shape, NEG, F32)
    l_sc[...] = jnp.zeros(l_sc.shape, F32)
    acc_sc[...] = jnp.zeros(acc_sc.shape, F32)
    key = lax.broadcasted_iota(I32, (CHUNK, tq), 0)
    qry = lax.broadcasted_iota(I32, (CHUNK, tq), 1)

    def step(j, masked):
        scores = []
        for hh in range(N_ATT_HEADS):
            xs = slice(2 * hh * dh, (2 * hh + 2) * dh)
            scores.append(_dot_nt(kx_sc[j, :, xs], qx_sc[:, xs]))
        probs, alphas = [], []
        for hh in range(N_ATT_HEADS):
            s = scores[hh]
            if masked:
                s = jnp.where(j * CHUNK + key <= ti * tq + qry, s, NEG)
            m_old = m_sc[hh]
            m_new = jnp.maximum(m_old, jnp.max(s, axis=0, keepdims=True))
            p = jnp.exp(s - m_new)
            alpha = jnp.exp(m_old - m_new)
            l_sc[hh] = alpha * l_sc[hh] + jnp.sum(p, axis=0, keepdims=True)
            m_sc[hh] = m_new
            probs.append(p.astype(BF16))
            alphas.append(alpha)
        for hh in range(N_ATT_HEADS):
            acc_sc[hh] = alphas[hh] * acc_sc[hh] + _dot(vt_sc[hh, j], probs[hh])

    def body(j, carry):
        step(j, False)
        return carry

    per = tq // CHUNK
    lax.fori_loop(0, ti * per, body, 0)
    for dj in range(per):
        j = ti * per + dj

        @pl.when(j < n_chunks)
        def _():
            step(j, True)

    for hh in range(N_ATT_HEADS):
        sl = slice(hh * dh, (hh + 1) * dh)
        o_ref[0, :, sl] = (acc_sc[hh] / l_sc[hh]).T.astype(o_ref.dtype)


def _fox_prompt(q, q_aug, kb, k_aug, vb):
    nb, lp, d_att = q.shape
    nc = lp // CHUNK
    tq = FOX_TQ
    blk = pl.BlockSpec((1, tq, d_att), lambda b, i: (b, i, 0))
    full = pl.BlockSpec((1, lp, d_att), lambda b, i: (b, 0, 0))
    nh, dh = N_ATT_HEADS, ATT_HEAD_DIM
    return pl.pallas_call(
        functools.partial(_fox_kernel, n_chunks=nc, tq=tq),
        grid=(nb, pl.cdiv(lp, tq)),
        in_specs=[blk, blk, full, full, full],
        out_specs=blk,
        out_shape=jax.ShapeDtypeStruct((nb, lp, d_att), BF16),
        scratch_shapes=[pltpu.VMEM((nc, CHUNK, 2 * d_att), BF16),
                        pltpu.VMEM((nh, nc, dh, CHUNK), BF16),
                        pltpu.VMEM((tq, 2 * d_att), BF16),
                        pltpu.VMEM((nh, 1, tq), F32), pltpu.VMEM((nh, 1, tq), F32),
                        pltpu.VMEM((nh, dh, tq), F32)],
        compiler_params=_cparams("parallel", "arbitrary"),
        name="fox_prompt",
    )(q, q_aug, kb, k_aug, vb)


def _mlstm_kernel(q_ref, k_ref, v_ref, g_ref, cl_ref, gt_ref, clt_ref,
                  h_ref, c_out_ref, n_out_ref, m_out_ref, c_sc, n_sc, m_sc):
    ci = pl.program_id(1)

    @pl.when(ci == 0)
    def _():
        c_sc[...] = jnp.zeros(c_sc.shape, F32)
        n_sc[...] = jnp.zeros(n_sc.shape, F32)
        m_sc[...] = jnp.zeros(m_sc.shape, F32)

    r = lax.broadcasted_iota(I32, (CHUNK, CHUNK), 0)
    c = lax.broadcasted_iota(I32, (CHUNK, CHUNK), 1)
    causal = c <= r
    last = slice(CHUNK - 1, CHUNK)
    for hh in range(N_MLSTM_HEADS):
        qk = slice(hh * MLSTM_QK_DIM, (hh + 1) * MLSTM_QK_DIM)
        vv = slice(hh * MLSTM_V_DIM, (hh + 1) * MLSTM_V_DIM)
        ig_l, lf_l = GATE_IG + hh, GATE_LF_M + hh
        b_col = cl_ref[0, :, lf_l:lf_l + 1]
        ig_col = g_ref[0, :, ig_l:ig_l + 1]
        b_row = clt_ref[0, 0, lf_l:lf_l + 1, :]
        ig_row = gt_ref[0, 0, ig_l:ig_l + 1, :]
        m0 = m_sc[hh][:, 0:1]
        c0 = c_sc[hh]
        n0 = n_sc[hh]
        qh = q_ref[0, :, qk]
        kh = k_ref[0, :, qk]
        vh = v_ref[0, :, vv]

        dmat = jnp.where(causal, b_col - b_row + ig_row, NEG)
        inter = b_col + m0
        mt = jnp.maximum(inter, jnp.max(dmat, axis=-1, keepdims=True))
        w_intra = jnp.exp(dmat - mt)
        w_inter = jnp.exp(inter - mt)
        s = _dot_nt(qh, kh) * w_intra
        num = _dot(s.astype(BF16), vh) + w_inter * _dot(qh, c0.astype(BF16))
        den = jnp.sum(s, axis=-1, keepdims=True) + w_inter * jnp.sum(qh.astype(F32) * n0, axis=-1, keepdims=True)
        h_ref[0, :, vv] = num / jnp.maximum(jnp.abs(den), jnp.exp(-mt))

        m_new = mt[last, :]
        b_last = b_col[last, :]
        w_state = jnp.exp(b_last - b_col + ig_col - m_new)
        decay = jnp.exp(b_last + m0 - m_new)
        kw = kh.astype(F32) * w_state
        c_new = decay * c0 + _dot_tn(kw.astype(BF16), vh)
        n_new = decay * n0 + jnp.sum(kw, axis=0, keepdims=True)
        c_sc[hh] = c_new
        n_sc[hh] = n_new
        m_sc[hh] = jnp.broadcast_to(m_new, (1, LANES))
        c_out_ref[0, hh] = c_new
        n_out_ref[0, hh] = n_new
        m_out_ref[0, hh] = jnp.broadcast_to(m_new, (1, LANES))


def _mlstm_prompt(qm, km, vm, gates, cum_l, gates_t, cum_lt):
    nb, lp, d_mqk = qm.shape
    d_mv = vm.shape[-1]
    nc = lp // CHUNK
    hm = N_MLSTM_HEADS
    blk = lambda w: pl.BlockSpec((1, CHUNK, w), lambda b, i: (b, i, 0))
    tblk = pl.BlockSpec((1, 1, LANES, CHUNK), lambda b, i: (b, i, 0, 0))
    return pl.pallas_call(
        _mlstm_kernel,
        grid=(nb, nc),
        in_specs=[blk(d_mqk), blk(d_mqk), blk(d_mv), blk(LANES), blk(LANES), tblk, tblk],
        out_specs=(blk(d_mv),
                   pl.BlockSpec((1, hm, MLSTM_QK_DIM, MLSTM_V_DIM), lambda b, i: (b, 0, 0, 0)),
                   pl.BlockSpec((1, hm, 1, MLSTM_QK_DIM), lambda b, i: (b, 0, 0, 0)),
                   pl.BlockSpec((1, hm, 1, LANES), lambda b, i: (b, 0, 0, 0))),
        out_shape=(jax.ShapeDtypeStruct((nb, lp, d_mv), F32),
                   jax.ShapeDtypeStruct((nb, hm, MLSTM_QK_DIM, MLSTM_V_DIM), F32),
                   jax.ShapeDtypeStruct((nb, hm, 1, MLSTM_QK_DIM), F32),
                   jax.ShapeDtypeStruct((nb, hm, 1, LANES), F32)),
        scratch_shapes=[pltpu.VMEM((hm, MLSTM_QK_DIM, MLSTM_V_DIM), F32),
                        pltpu.VMEM((hm, 1, MLSTM_QK_DIM), F32),
                        pltpu.VMEM((hm, 1, LANES), F32)],
        compiler_params=_cparams("parallel", "arbitrary"),
        name="mlstm_prompt",
    )(qm, km, vm, gates, cum_l, gates_t, cum_lt)


def _paged_kernel(pt_ref, q_ref, kn_ref, vn_ref, lfn_ref, *refs, pps):
    k_refs = refs[:pps]
    v_refs = refs[pps:2 * pps]
    lf_refs = refs[2 * pps:3 * pps]
    o_ref, lf_sc, bias_sc, m_sc, l_sc, acc_sc, carry_sc = refs[3 * pps:]
    g = pl.program_id(1)
    nh, dh = N_ATT_HEADS, ATT_HEAD_DIM
    page = k_refs[0].shape[0]
    width = page * nh
    q8 = q_ref[0]

    @pl.when(g == 0)
    def _():
        m_sc[...] = jnp.sum(q8 * kn_ref[0], axis=-1, keepdims=True)
        l_sc[...] = jnp.ones(l_sc.shape, F32)
        acc_sc[...] = vn_ref[0]
        carry_sc[...] = jnp.zeros(carry_sc.shape, F32)

    for u in range(pps):
        lf_sc[u:u + 1, :] = lf_refs[u][...]
    lf = lf_sc[...]
    lane = lax.broadcasted_iota(I32, (pps, width), 1)
    incl = lf
    k = nh
    while k < width:
        incl = incl + jnp.where(lane < width - k, pltpu.roll(incl, width - k, axis=1), 0.0)
        k *= 2
    tot = jnp.where(lane < nh, incl, 0.0)
    k = nh
    while k < width:
        tot = tot + pltpu.roll(tot, k, axis=1)
        k *= 2
    run = carry_sc[...]
    for u in range(pps):
        bias_sc[u:u + 1, :] = run
        run = run + tot[u:u + 1, :]
    carry_sc[...] = run
    bias = bias_sc[...] + (incl - lf) + lfn_ref[0]

    own = lax.broadcasted_iota(I32, (nh, width), 1) % nh == lax.broadcasted_iota(I32, (nh, width), 0)
    q_bf = q8.astype(BF16)
    scores = []
    m_old = m_sc[...]
    m_new = m_old
    for u in range(pps):
        kf = k_refs[u][...].reshape(width, dh).astype(BF16)
        s = jnp.where(own, _dot_nt(q_bf, kf) + bias[u:u + 1, :], NEG)
        m_new = jnp.maximum(m_new, jnp.max(s, axis=-1, keepdims=True))
        scores.append(s)
    alpha = jnp.exp(m_old - m_new)
    l_new = alpha * l_sc[...]
    acc = alpha * acc_sc[...]
    for u in range(pps):
        p = jnp.exp(scores[u] - m_new)
        l_new = l_new + jnp.sum(p, axis=-1, keepdims=True)
        acc = acc + _dot(p.astype(BF16), v_refs[u][...].reshape(width, dh).astype(BF16))
    m_sc[...] = m_new
    l_sc[...] = l_new
    acc_sc[...] = acc

    @pl.when(g == pl.num_programs(1) - 1)
    def _():
        o_ref[0] = acc / l_new


def _paged_attention(q, k_new, v_new, lf_new_flat, cache_k, cache_v, cache_lf_flat, page_table):
    db, n_pages = page_table.shape
    _, n_pool, page, nh, dh = cache_k.shape
    width = page * nh
    pps = PAGES_PER_STEP
    steps = n_pages // pps
    tok = lambda: pl.BlockSpec((1, nh, dh), lambda b, g, pt: (b, 0, 0))

    def page_id(b, g, pt, u):
        return pt[b * n_pages + (n_pages - 1 - (g * pps + u))]

    kv_specs = [pl.BlockSpec((None, None, page, nh, dh),
                             functools.partial(lambda b, g, pt, u: (0, page_id(b, g, pt, u), 0, 0, 0), u=u))
                for u in range(pps)]
    lf_specs = [pl.BlockSpec((None, 1, width),
                             functools.partial(lambda b, g, pt, u: (page_id(b, g, pt, u), 0, 0), u=u))
                for u in range(pps)]
    grid_spec = pltpu.PrefetchScalarGridSpec(
        num_scalar_prefetch=1,
        grid=(db, steps),
        in_specs=[tok(), tok(), tok(), pl.BlockSpec((1, 1, width), lambda b, g, pt: (b, 0, 0))]
        + kv_specs + kv_specs + lf_specs,
        out_specs=tok(),
        scratch_shapes=[pltpu.VMEM((pps, width), F32), pltpu.VMEM((pps, width), F32),
                        pltpu.VMEM((nh, 1), F32), pltpu.VMEM((nh, 1), F32),
                        pltpu.VMEM((nh, dh), F32), pltpu.VMEM((1, width), F32)],
    )
    return pl.pallas_call(
        functools.partial(_paged_kernel, pps=pps),
        grid_spec=grid_spec,
        out_shape=jax.ShapeDtypeStruct((db, nh, dh), F32),
        compiler_params=_cparams("parallel", "arbitrary"),
        name="paged_fox",
    )(page_table.reshape(-1), q, k_new, v_new, lf_new_flat, *([cache_k] * pps), *([cache_v] * pps),
      *([cache_lf_flat] * pps))


def _mstep_kernel(q_ref, k_ref, v_ref, ig_ref, lf_ref, c_ref, n_ref, m_ref,
                  h_ref, c_out_ref, n_out_ref, m_out_ref):
    for hh in range(N_MLSTM_HEADS):
        q = q_ref[0, hh].astype(F32)
        k = k_ref[0, hh].astype(F32)
        v = v_ref[0, hh].astype(F32)
        ig = ig_ref[0, hh]
        lf = lf_ref[0, hh]
        c0 = c_ref[0, hh]
        n0 = n_ref[0, hh]
        m0 = m_ref[0, hh]
        inter = lf + m0
        m = jnp.maximum(inter, ig)
        w_intra = jnp.exp(ig - m)
        w_inter = jnp.exp(inter - m)
        s = jnp.sum(q * k, axis=0, keepdims=True) * w_intra
        num = s * v + w_inter * jnp.sum(q * c0, axis=0, keepdims=True)
        den = s + w_inter * jnp.sum(q * n0, axis=0, keepdims=True)
        h_ref[0, hh] = num / jnp.maximum(jnp.abs(den), jnp.exp(-m))
        c_out_ref[0, hh] = w_inter * c0 + w_intra * (k * v)
        n_out_ref[0, hh] = w_inter * n0 + w_intra * k
        m_out_ref[0, hh] = m


def _mlstm_step(q_col, k_col, v_row, ig, lf, state_c, state_n_col, state_m):
    db, hm, dk, dv = state_c.shape
    spec = lambda a, b_: pl.BlockSpec((1, hm, a, b_), lambda i: (i, 0, 0, 0))
    return pl.pallas_call(
        _mstep_kernel,
        grid=(db,),
        in_specs=[spec(dk, 1), spec(dk, 1), spec(1, dv), spec(1, 1), spec(1, 1), spec(dk, dv), spec(dk, 1), spec(1, 1)],
        out_specs=(spec(1, dv), spec(dk, dv), spec(dk, 1), spec(1, 1)),
        out_shape=(jax.ShapeDtypeStruct((db, hm, 1, dv), F32), jax.ShapeDtypeStruct((db, hm, dk, dv), F32),
                   jax.ShapeDtypeStruct((db, hm, dk, 1), F32), jax.ShapeDtypeStruct((db, hm, 1, 1), F32)),
        compiler_params=_cparams("parallel"),
        name="mlstm_step",
    )(q_col, k_col, v_row, ig, lf, state_c, state_n_col, state_m)


def _mix_kernel(att_ref, hm_ref, og_ref, h_ref, gm_ref, w_ref, lg_ref, lb_ref, wrh_ref, wrl_ref, br_ref,
                h1_ref, h1p_ref, e_ref, gate_ref, *, alpha):
    d_att = att_ref.shape[-1]
    parts = []
    for hh in range(N_MLSTM_HEADS):
        vv = slice(hh * MLSTM_V_DIM, (hh + 1) * MLSTM_V_DIM)
        x = hm_ref[:, vv]
        xn = x * lax.rsqrt(jnp.mean(x * x, axis=-1, keepdims=True) + RMS_EPS) * gm_ref[:, vv]
        parts.append((og_ref[:, vv] * xn).astype(BF16))
    mix = _dot(att_ref[...].astype(BF16), w_ref[0:d_att, :])
    for hh in range(N_MLSTM_HEADS):
        r0 = d_att + hh * MLSTM_V_DIM
        mix = mix + _dot(parts[hh], w_ref[r0:r0 + MLSTM_V_DIM, :])
    h1 = _layer_norm(alpha * h_ref[...] + mix, lg_ref[...], lb_ref[...])
    h1_ref[...] = h1
    half = h1.shape[-1] // 2
    hi = lax.bitcast_convert_type(h1[:, :half].astype(BF16).astype(F32), jnp.uint32)
    lo = lax.bitcast_convert_type(h1[:, half:].astype(BF16).astype(F32), jnp.uint32)
    h1p_ref[...] = hi | (lo >> 16)

    logits = _dot3(h1, wrh_ref[...], wrl_ref[...]) + br_ref[...]
    lane = lax.broadcasted_iota(I32, logits.shape, 1).astype(F32)
    e_out = jnp.zeros(logits.shape, F32)
    top = []
    for kk in range(TOP_K):
        mx = jnp.max(logits, axis=-1, keepdims=True)
        idx = jnp.min(jnp.where(logits == mx, lane, float(LANES)), axis=-1, keepdims=True)
        logits = jnp.where(lane == idx, -jnp.inf, logits)
        e_out = jnp.where(lane == float(kk), idx, e_out)
        top.append(mx)
    ex = [jnp.exp(t - top[0]) for t in top]
    inv = 1.0 / (ex[0] + ex[1] + ex[2] + ex[3])
    gate = jnp.zeros(logits.shape, F32)
    for kk in range(TOP_K):
        gate = jnp.where(lane == float(kk), ex[kk] * inv, gate)
    e_ref[...] = e_out.astype(I32)
    gate_ref[...] = gate


def _mix(att, hm, og, h, prm, tm, alpha):
    rows, d = h.shape
    d_att = att.shape[-1]
    d_mv = hm.shape[-1]
    row = lambda w: pl.BlockSpec((tm, w), lambda i: (i, 0))
    const = lambda r, c: pl.BlockSpec((r, c), lambda i: (0, 0), pipeline_mode=pl.Buffered(1))
    return pl.pallas_call(
        functools.partial(_mix_kernel, alpha=alpha),
        grid=(rows // tm,),
        in_specs=[row(d_att), row(d_mv), row(d_mv), row(d), const(1, d_mv), const(d_att + d_mv, d),
                  const(1, d), const(1, d), const(d, LANES), const(d, LANES), const(1, LANES)],
        out_specs=(row(d), row(d // 2), row(LANES), row(LANES)),
        out_shape=(jax.ShapeDtypeStruct((rows, d), F32), jax.ShapeDtypeStruct((rows, d // 2), jnp.uint32),
                   jax.ShapeDtypeStruct((rows, LANES), I32), jax.ShapeDtypeStruct((rows, LANES), F32)),
        compiler_params=_cparams("parallel"),
        name="mix",
    )(att, hm, og, h, prm["g_m_out"], prm["w_out"], prm["ln1_g"], prm["ln1_b"], prm["wr_hi"], prm["wr_lo"],
      prm["b_router"])


def _moe_kernel(ie_ref, ns_ref, iv_ref, st_ref, sp_ref,
                x_hbm, wg_ref, wu_ref, bg_ref, bu_ref, wd_ref, bd_ref,
                y_hbm, xbuf, xb, acc, gsem, ssem, *, n_tok, n_items, n_ft):
    i = pl.program_id(0)
    f = pl.program_id(1)
    nf = pl.num_programs(1)
    slot = i % 2
    other = 1 - slot
    n_sub = ns_ref[i]
    valid = iv_ref[i] == 1
    half = xbuf.shape[-1]
    dump_base = TOP_K * n_tok

    def gather_rows(item, sl, row0, count):
        base = st_ref[item] + row0
        dst = xbuf.at[sl, pl.ds(row0, count), :]
        for r in range(count):
            pair = sp_ref[base + r]
            tok = jnp.maximum(pair, 0) >> LOG2_TOP_K
            pltpu.make_async_copy(x_hbm.at[pl.ds(tok, 1), :], dst.at[pl.ds(r, 1), :], gsem.at[sl]).start()

    def scatter_rows(item, sl, row0, count):
        base = st_ref[item] + row0
        n_real = ns_ref[item] * MOE_SUB - row0
        dump = dump_base + sl * MOE_ROWS + row0
        src = acc.at[sl, pl.ds(row0, count), :]
        for r in range(count):
            pair = sp_ref[base + r]
            real = jnp.logical_and(pair >= 0, r < n_real)
            dst = jnp.where(real, (pair & (TOP_K - 1)) * n_tok + (pair >> LOG2_TOP_K), dump + r)
            pltpu.make_async_copy(src.at[pl.ds(r, 1), :], y_hbm.at[pl.ds(dst, 1), :], ssem.at[sl]).start()

    def gather_wait(sl):
        pltpu.make_async_copy(x_hbm.at[pl.ds(0, MOE_ROWS), :], xbuf.at[sl], gsem.at[sl]).wait()

    def scatter_wait(sl):
        pltpu.make_async_copy(acc.at[sl], y_hbm.at[pl.ds(0, MOE_ROWS), :], ssem.at[sl]).wait()

    def all_steps(fn):
        def body(step, carry):
            fn(pl.multiple_of(step * MOE_SUB, MOE_SUB))
            return carry
        lax.fori_loop(0, MOE_ROWS // MOE_SUB, body, 0)

    nxt = jnp.minimum(i + 1, n_items - 1)
    has_next = jnp.logical_and(i + 1 < n_items, iv_ref[nxt] == 1)
    prv = jnp.maximum(i - 1, 0)
    has_prev = jnp.logical_and(i >= 1, iv_ref[prv] == 1)
    prv2 = jnp.maximum(i - 2, 0)
    has_prev2 = jnp.logical_and(i >= 2, iv_ref[prv2] == 1)

    @pl.when(f == 0)
    def _():
        @pl.when(jnp.logical_and(i == 0, valid))
        def _():
            all_steps(lambda row0: gather_rows(0, 0, row0, MOE_SUB))

        @pl.when(valid)
        def _():
            gather_wait(slot)
            packed = xbuf[slot]
            hi = lax.bitcast_convert_type(packed & jnp.uint32(0xFFFF0000), F32)
            lo = lax.bitcast_convert_type(packed << 16, F32)
            xb[:, 0:half] = hi.astype(BF16)
            xb[:, half:2 * half] = lo.astype(BF16)

        @pl.when(has_prev2)
        def _():
            scatter_wait(slot)

        @pl.when(valid)
        def _():
            acc[slot] = jnp.broadcast_to(bd_ref[0], acc.shape[1:])

    per_step = MOE_ROWS // n_ft
    row0 = pl.multiple_of(f * per_step, per_step)

    @pl.when(has_next)
    def _():
        gather_rows(i + 1, other, row0, per_step)

    @pl.when(has_prev)
    def _():
        scatter_rows(i - 1, other, row0, per_step)

    @pl.when(valid)
    def _():
        wg = wg_ref[0].astype(BF16)
        wu = wu_ref[0].astype(BF16)
        wd = wd_ref[0].astype(BF16)

        def rows_block(start, count):
            rows = pl.ds(start, count)
            xs = xb[rows, :]
            gp = _dot(xs, wg) + bg_ref[0]
            up = _dot(xs, wu) + bu_ref[0]
            gp = jnp.minimum(gp, SWIGLU_LIMIT)
            up = jnp.clip(up, -SWIGLU_LIMIT, SWIGLU_LIMIT)
            act = gp * _sigmoid(SWIGLU_ALPHA * gp) * (up + 1.0)
            acc[slot, rows, :] += _dot(act.astype(BF16), wd)

        per_block = MOE_BLOCK // MOE_SUB
        n_full = n_sub // per_block

        def body(blk, carry):
            rows_block(pl.multiple_of(blk * MOE_BLOCK, MOE_BLOCK), MOE_BLOCK)
            return carry

        lax.fori_loop(0, n_full, body, 0)
        tail = n_sub - n_full * per_block
        for nn in range(1, per_block):
            @pl.when(tail == nn)
            def _():
                rows_block(pl.multiple_of(n_full * MOE_BLOCK, MOE_BLOCK), nn * MOE_SUB)

    @pl.when(jnp.logical_and(i == n_items - 1, f == nf - 1))
    def _():
        @pl.when(valid)
        def _():
            all_steps(lambda row0: scatter_rows(i, slot, row0, MOE_SUB))
            scatter_wait(slot)

        @pl.when(has_prev)
        def _():
            scatter_wait(other)


def _moe(x_packed, item_e, item_ns, item_valid, item_start, slot_pair, w_gate_up, b_gate_up, w_down, b_down):
    n_tok, half = x_packed.shape
    d = 2 * half
    n_exp, _, two_f = w_gate_up.shape
    nf = two_f // 2 // MOE_FT
    n_items = item_e.shape[0]
    assert MOE_ROWS % nf == 0

    def ftile(i, f, iv):
        return f * iv[i] + (nf - 1) * (1 - iv[i])

    grid_spec = pltpu.PrefetchScalarGridSpec(
        num_scalar_prefetch=5,
        grid=(n_items, nf),
        in_specs=[
            pl.BlockSpec(memory_space=pl.ANY),
            pl.BlockSpec((1, d, MOE_FT), lambda i, f, ie, ns, iv, st, sp: (ie[i], 0, ftile(i, f, iv))),
            pl.BlockSpec((1, d, MOE_FT), lambda i, f, ie, ns, iv, st, sp: (ie[i], 0, nf + ftile(i, f, iv))),
            pl.BlockSpec((1, 1, MOE_FT), lambda i, f, ie, ns, iv, st, sp: (ie[i], 0, ftile(i, f, iv))),
            pl.BlockSpec((1, 1, MOE_FT), lambda i, f, ie, ns, iv, st, sp: (ie[i], 0, nf + ftile(i, f, iv))),
            pl.BlockSpec((1, MOE_FT, d), lambda i, f, ie, ns, iv, st, sp: (ie[i], ftile(i, f, iv), 0)),
            pl.BlockSpec((1, 1, d), lambda i, f, ie, ns, iv, st, sp: (ie[i], 0, 0)),
        ],
        out_specs=pl.BlockSpec(memory_space=pl.ANY),
        scratch_shapes=[pltpu.VMEM((2, MOE_ROWS, half), jnp.uint32), pltpu.VMEM((MOE_ROWS, d), BF16),
                        pltpu.VMEM((2, MOE_ROWS, d), F32),
                        pltpu.SemaphoreType.DMA((2,)), pltpu.SemaphoreType.DMA((2,))],
    )
    return pl.pallas_call(
        functools.partial(_moe_kernel, n_tok=n_tok, n_items=n_items, n_ft=nf),
        grid_spec=grid_spec,
        out_shape=jax.ShapeDtypeStruct((TOP_K * n_tok + 2 * MOE_ROWS, d), F32),
        compiler_params=_cparams("arbitrary", "arbitrary"),
        name="moe",
    )(item_e, item_ns, item_valid, item_start, slot_pair, x_packed, w_gate_up, w_gate_up,
      b_gate_up.reshape(n_exp, 1, two_f), b_gate_up.reshape(n_exp, 1, two_f), w_down, b_down.reshape(n_exp, 1, d))


def _combine_kernel(y0_ref, y1_ref, y2_ref, y3_ref, gate_ref, h1_ref, lg_ref, lb_ref, o_ref, *, alpha):
    acc = alpha * h1_ref[...]
    for kk, y_ref in enumerate((y0_ref, y1_ref, y2_ref, y3_ref)):
        acc = acc + y_ref[...] * gate_ref[:, kk:kk + 1]
    o_ref[0] = _layer_norm(acc, lg_ref[...], lb_ref[...])


def _combine(y_pairs, gate, h1, ln_g, ln_b, alpha, *, n_tok, tok0, group_rows, skip, out_rows, tm):
    d = h1.shape[-1]
    groups = h1.shape[0] // group_rows
    assert out_rows % tm == 0 and skip % 8 == 0 and group_rows % 8 == 0 and tok0 % 8 == 0 and n_tok % 8 == 0

    def rows(base):
        return lambda b, j: (pl.multiple_of(base + b * group_rows + skip + j * tm, 8), 0)

    def el(w, base):
        return pl.BlockSpec((pl.Element(tm), pl.Element(w)), rows(base))

    const = lambda c: pl.BlockSpec((1, c), lambda b, j: (0, 0))
    y_specs = [el(d, kk * n_tok + tok0) for kk in range(TOP_K)]
    return pl.pallas_call(
        functools.partial(_combine_kernel, alpha=alpha),
        grid=(groups, out_rows // tm),
        in_specs=y_specs + [el(LANES, 0), el(d, 0), const(d), const(d)],
        out_specs=pl.BlockSpec((1, tm, d), lambda b, j: (b, j, 0)),
        out_shape=jax.ShapeDtypeStruct((groups, out_rows, d), F32),
        compiler_params=_cparams("parallel", "parallel"),
        name="combine",
    )(*([y_pairs] * TOP_K), gate, h1, ln_g, ln_b)


def _route(top_e):
    n_pairs = top_e.shape[0] * TOP_K
    per_item = MOE_ROWS // MOE_SUB
    pair_e = top_e.reshape(-1)
    order = jnp.argsort(pair_e, stable=True).astype(I32)
    e_sorted = pair_e[order]
    edges = jnp.arange(N_EXPERTS + 1, dtype=I32)
    bounds = jnp.sum((e_sorted[None, :] < edges[:, None]).astype(I32), axis=1)
    start = bounds[:-1]
    counts = bounds[1:] - start
    n_sb = (counts + MOE_SUB - 1) // MOE_SUB
    sb_end = jnp.cumsum(n_sb)
    sb_start = sb_end - n_sb
    max_sb = n_pairs // MOE_SUB + N_EXPERTS
    slot = jnp.arange(max_sb * MOE_SUB, dtype=I32)
    e_slot = jnp.minimum(jnp.sum((slot[:, None] >= (sb_end * MOE_SUB)[None, :]).astype(I32), axis=1), N_EXPERTS - 1)
    rank = slot - sb_start[e_slot] * MOE_SUB
    src = jnp.clip(start[e_slot] + rank, 0, n_pairs - 1)
    slot_pair = jnp.where(rank < counts[e_slot], order[src], -1).astype(I32)
    slot_pair = jnp.concatenate([slot_pair, jnp.full((MOE_ROWS,), -1, I32)])

    n_it = (n_sb + per_item - 1) // per_item
    it_end = jnp.cumsum(n_it)
    it_start = it_end - n_it
    max_items = (max_sb + (per_item - 1) * N_EXPERTS) // per_item
    idx = jnp.arange(max_items, dtype=I32)
    n_valid = it_end[-1]
    item_valid = (idx < n_valid).astype(I32)
    idc = jnp.minimum(idx, n_valid - 1)
    item_e = jnp.minimum(jnp.sum((idc[:, None] >= it_end[None, :]).astype(I32), axis=1), N_EXPERTS - 1).astype(I32)
    chunk = idc - it_start[item_e]
    item_start = ((sb_start[item_e] + per_item * chunk) * MOE_SUB).astype(I32)
    item_ns = (jnp.clip(n_sb[item_e] - per_item * chunk, 0, per_item) * item_valid).astype(I32)
    return item_e, item_ns, item_valid, item_start, slot_pair


def _prep_params(ln_in_g, ln_in_b, w_in, b_att_f, g_att_q, g_att_k, b_m_i, b_m_f, g_m_out, w_out,
                 ln1_g, ln1_b, w_router, b_router, ln2_g, ln2_b):
    d = w_in.shape[0]
    d_att = N_ATT_HEADS * ATT_HEAD_DIM
    d_mqk = N_MLSTM_HEADS * MLSTM_QK_DIM
    d_mv = N_MLSTM_HEADS * MLSTM_V_DIM
    sizes = (d_att, d_att, d_att, N_ATT_HEADS, d_mqk, d_mqk, d_mv, N_MLSTM_HEADS, N_MLSTM_HEADS, d_mv)
    offs = [0]
    for s in sizes:
        offs.append(offs[-1] + s)
    cols = lambda i: w_in[:, offs[i]:offs[i + 1]]
    w_main = jnp.concatenate([cols(0), cols(1), cols(2), cols(4), cols(5), cols(6), cols(9)], axis=1).astype(BF16)
    w_gate = jnp.concatenate([cols(3), cols(7), cols(8), jnp.zeros((d, LANES - N_GATES), F32)], axis=1)
    wg_hi, wg_lo = _split2(w_gate)
    gate_bias = jnp.concatenate([b_att_f, b_m_i, b_m_f, jnp.zeros((LANES - N_GATES,), F32)]).reshape(1, LANES)
    w_r = jnp.concatenate([w_router, jnp.zeros((d, LANES - N_EXPERTS), F32)], axis=1)
    wr_hi, wr_lo = _split2(w_r)
    b_r = jnp.concatenate([b_router, jnp.full((LANES - N_EXPERTS,), NEG, F32)]).reshape(1, LANES)
    return dict(
        ln_in_g=ln_in_g.reshape(1, d), ln_in_b=ln_in_b.reshape(1, d), w_main=w_main, wg_hi=wg_hi, wg_lo=wg_lo,
        gate_bias=gate_bias, g_att_q=g_att_q.reshape(1, -1), g_att_k=g_att_k.reshape(1, -1),
        g_m_out=g_m_out.reshape(1, -1), w_out=w_out.astype(BF16), ln1_g=ln1_g.reshape(1, d),
        ln1_b=ln1_b.reshape(1, d), wr_hi=wr_hi, wr_lo=wr_lo, b_router=b_r,
        ln2_g=ln2_g.reshape(1, d), ln2_b=ln2_b.reshape(1, d))


def kernel(x_prompt, x_sample, cache_k, cache_v, cache_logf, state_C, state_n, state_m, page_table,
           meta_tokens, ln_in_g, ln_in_b, w_in, b_att_f, g_att_q, g_att_k, b_m_i, b_m_f, g_m_out, w_out,
           ln1_g, ln1_b, w_router, b_router, w_gate_up, b_gate_up, w_down, b_down, ln2_g, ln2_b):
    depth = w_in.shape[0]
    assert depth == 1, "single-layer trunk only"
    nb, seq, d = x_prompt.shape
    db = x_sample.shape[0]
    assert x_sample.shape[1] == 1, "one new token per sample"
    alpha = (2.0 * depth) ** 0.25
    d_att = N_ATT_HEADS * ATT_HEAD_DIM
    hm = N_MLSTM_HEADS
    prm = _prep_params(ln_in_g, ln_in_b, w_in[0], b_att_f[0], g_att_q[0], g_att_k[0], b_m_i[0], b_m_f[0],
                       g_m_out[0], w_out[0], ln1_g[0], ln1_b[0], w_router[0], b_router[0], ln2_g[0], ln2_b[0])

    length = N_META + seq
    lp = pl.cdiv(length, CHUNK) * CHUNK
    meta = jnp.broadcast_to(meta_tokens[None], (nb, N_META, d))
    xp = jnp.concatenate([meta, x_prompt, jnp.zeros((nb, lp - length, d), F32)], axis=1)
    (hp, qa, k32, kb, v32, vb, gates, qm, km, vm, og) = _proj(xp, length, lp // 8, prm)
    cum_l, cum_lt, gates_t, k_aug, q_aug = _gate_scan(gates)
    att_p = _fox_prompt(qa, q_aug, kb, k_aug, vb)
    hm_p, c_p, n_p, m_p = _mlstm_prompt(qm, km, vm, gates, cum_l, gates_t, cum_lt)
    rows_p = nb * lp
    h1_p, h1p_p, e_p, gate_p = _mix(att_p.reshape(rows_p, d_att), hm_p.reshape(rows_p, -1), og.reshape(rows_p, -1),
                                    hp.reshape(rows_p, d), prm, 256, alpha)

    xs = x_sample.reshape(1, db, d)
    (hs, qa_s, k32_s, _, v32_s, _, gates_s, qm_s, km_s, vm_s, og_s) = _proj(xs, db, db, prm)
    n_pool, page = cache_k.shape[1], cache_k.shape[2]
    nh, dh = N_ATT_HEADS, ATT_HEAD_DIM
    lf_s = gates_s[0, :, GATE_LF_ATT:GATE_LF_ATT + nh]
    att_s = _paged_attention(
        qa_s.astype(F32).reshape(db, nh, dh), k32_s.reshape(db, nh, dh), v32_s.reshape(db, nh, dh),
        jnp.tile(lf_s, (1, page)).reshape(db, 1, page * nh), cache_k, cache_v,
        cache_logf[0].reshape(n_pool, 1, page * nh), page_table)
    ig_s = gates_s[0, :, GATE_IG:GATE_IG + hm].reshape(db, hm, 1, 1)
    lfm_s = gates_s[0, :, GATE_LF_M:GATE_LF_M + hm].reshape(db, hm, 1, 1)
    hm_s, c_s, n_s, m_s = _mlstm_step(
        qm_s.astype(F32).reshape(db, hm, MLSTM_QK_DIM, 1), km_s.astype(F32).reshape(db, hm, MLSTM_QK_DIM, 1),
        vm_s.astype(F32).reshape(db, hm, 1, MLSTM_V_DIM), ig_s, lfm_s, state_C[0],
        state_n[0].reshape(db, hm, MLSTM_QK_DIM, 1), state_m[0].reshape(db, hm, 1, 1))
    h1_s, h1p_s, e_s, gate_s = _mix(att_s.reshape(db, d_att), hm_s.reshape(db, -1), og_s.reshape(db, -1),
                                    hs.reshape(db, d), prm, db, alpha)

    n_tok = rows_p + db
    x_packed = jnp.concatenate([h1p_p, h1p_s], axis=0)
    top_e = jnp.concatenate([e_p, e_s], axis=0)[:, :TOP_K]
    item_e, item_ns, item_valid, item_start, slot_pair = _route(top_e)
    y_pairs = _moe(x_packed, item_e, item_ns, item_valid, item_start, slot_pair,
                   w_gate_up[0], b_gate_up[0], w_down[0], b_down[0])
    y_prompt = _combine(y_pairs, gate_p, h1_p, prm["ln2_g"], prm["ln2_b"], alpha, n_tok=n_tok, tok0=0,
                        group_rows=lp, skip=N_META, out_rows=seq, tm=256)
    y_sample = _combine(y_pairs, gate_s, h1_s, prm["ln2_g"], prm["ln2_b"], alpha, n_tok=n_tok, tok0=rows_p,
                        group_rows=db, skip=0, out_rows=db, tm=db).reshape(db, 1, d)
    k_prompt = k32.reshape(1, nb, length, N_ATT_HEADS, ATT_HEAD_DIM)
    v_prompt = v32.reshape(1, nb, length, N_ATT_HEADS, ATT_HEAD_DIM)
    lf_prompt = gates[:, :length, GATE_LF_ATT:GATE_LF_ATT + N_ATT_HEADS][None]
    return (y_prompt, y_sample, k_prompt, v_prompt, lf_prompt,
            c_p[None], n_p.reshape(1, nb, hm, MLSTM_QK_DIM), m_p[:, :, 0, 0][None],
            k32_s.reshape(1, db, 1, N_ATT_HEADS, ATT_HEAD_DIM), v32_s.reshape(1, db, 1, N_ATT_HEADS, ATT_HEAD_DIM),
            lf_s.reshape(1, db, 1, nh),
            c_s[None], n_s.reshape(1, db, hm, MLSTM_QK_DIM), m_s.reshape(1, db, hm))
```

```python
import functools

import jax
import jax.numpy as jnp
from jax import lax
from jax.experimental import pallas as pl
from jax.experimental.pallas import tpu as pltpu

F32 = jnp.float32
BF16 = jnp.bfloat16
I32 = jnp.int32

N_META = 16
CHUNK = 128
ATT_HEAD_DIM = 128
N_ATT_HEADS = 8
N_MLSTM_HEADS = 4
MLSTM_QK_DIM = 128
MLSTM_V_DIM = 256
N_EXPERTS = 32
TOP_K = 4
LOG2_TOP_K = 2
SWIGLU_ALPHA = 1.702
SWIGLU_LIMIT = 7.0
LN_EPS = 1e-5
RMS_EPS = 1e-6
NEG = -1e30

LANES = 128
GATE_LF_ATT = 0
GATE_IG = N_ATT_HEADS
GATE_LF_M = GATE_IG + N_MLSTM_HEADS
N_GATES = GATE_LF_M + N_MLSTM_HEADS

VMEM_LIMIT = 56 * 1024 * 1024

MOE_ROWS = 1152
MOE_SUB = 128
MOE_FT = 256
MOE_BLOCK = 512
PAGES_PER_STEP = 8
FOX_TQ = 256


def _cparams(*sem):
    return pltpu.CompilerParams(dimension_semantics=sem, vmem_limit_bytes=VMEM_LIMIT)


def _split2(x):
    hi = x.astype(BF16)
    lo = (x - hi.astype(F32)).astype(BF16)
    return hi, lo


def _split3(x):
    hi = x.astype(BF16)
    r1 = x - hi.astype(F32)
    mid = r1.astype(BF16)
    lo = (r1 - mid.astype(F32)).astype(BF16)
    return hi, mid, lo


def _dot(a, b):
    return jnp.dot(a, b, preferred_element_type=F32)


def _dot_nt(a, b):
    return lax.dot_general(a, b, (((1,), (1,)), ((), ())), preferred_element_type=F32)


def _dot_tn(a, b):
    return lax.dot_general(a, b, (((0,), (0,)), ((), ())), preferred_element_type=F32)


def _dot3(x_f32, w_hi, w_lo):
    x_hi, x_lo = _split2(x_f32)
    return _dot(x_hi, w_hi) + (_dot(x_lo, w_hi) + _dot(x_hi, w_lo))


def _layer_norm(x, g, b):
    mu = jnp.mean(x, axis=-1, keepdims=True)
    xc = x - mu
    var = jnp.mean(xc * xc, axis=-1, keepdims=True)
    return xc * lax.rsqrt(var + LN_EPS) * g + b


def _log_sigmoid(z):
    return jnp.minimum(z, 0.0) - jnp.log(1.0 + jnp.exp(-jnp.abs(z)))


def _sigmoid(z):
    return 1.0 / (1.0 + jnp.exp(-z))


def _proj_kernel(x_ref, lng_ref, lnb_ref, w_ref, wgh_ref, wgl_ref, gb_ref, gq_ref, gk_ref,
                 h_ref, q_ref, k32_ref, kb_ref, v32_ref, vb_ref, g_ref, qm_ref, km_ref, vm_ref, og_ref,
                 *, tm, valid_len):
    d_att = N_ATT_HEADS * ATT_HEAD_DIM
    d_mqk = N_MLSTM_HEADS * MLSTM_QK_DIM
    d_mv = N_MLSTM_HEADS * MLSTM_V_DIM
    h = _layer_norm(x_ref[0], lng_ref[...], lnb_ref[...])
    h_ref[0] = h
    hb = h.astype(BF16)

    def seg(c0, width):
        return _dot(hb, w_ref[:, c0:c0 + width])

    scale = ATT_HEAD_DIM ** -0.5
    qa = seg(0, d_att)
    ka = seg(d_att, d_att)
    for hh in range(N_ATT_HEADS):
        sl = slice(hh * ATT_HEAD_DIM, (hh + 1) * ATT_HEAD_DIM)
        qh = qa[:, sl]
        qn = qh * lax.rsqrt(jnp.mean(qh * qh, axis=-1, keepdims=True) + RMS_EPS) * gq_ref[...]
        q_ref[0, :, sl] = (qn * scale).astype(BF16)
        kh = ka[:, sl]
        kn = kh * lax.rsqrt(jnp.mean(kh * kh, axis=-1, keepdims=True) + RMS_EPS) * gk_ref[...]
        k32_ref[0, :, hh, :] = kn
        kb_ref[0, :, sl] = kn.astype(BF16)
    va = seg(2 * d_att, d_att)
    for hh in range(N_ATT_HEADS):
        v32_ref[0, :, hh, :] = va[:, hh * ATT_HEAD_DIM:(hh + 1) * ATT_HEAD_DIM]
    vb_ref[0] = va.astype(BF16)
    c0 = 3 * d_att
    qm_ref[0] = seg(c0, d_mqk).astype(BF16)
    km_ref[0] = (seg(c0 + d_mqk, d_mqk) * (MLSTM_QK_DIM ** -0.5)).astype(BF16)
    vm_ref[0] = seg(c0 + 2 * d_mqk, d_mv).astype(BF16)
    og_ref[0] = _sigmoid(seg(c0 + 2 * d_mqk + d_mv, d_mv))

    z = _dot3(h, wgh_ref[...], wgl_ref[...]) + gb_ref[...]
    lane = lax.broadcasted_iota(I32, z.shape, 1)
    row = pl.program_id(1) * tm + lax.broadcasted_iota(I32, z.shape, 0)
    is_ig = jnp.logical_and(lane >= GATE_IG, lane < GATE_LF_M)
    gates = jnp.where(is_ig, z, jnp.where(lane < N_GATES, _log_sigmoid(z), 0.0))
    gates = jnp.where(row < valid_len, gates, jnp.where(is_ig, NEG, 0.0))
    g_ref[0] = gates


def _proj(x, valid_len, tm, prm):
    nb, lp, d = x.shape
    d_att = N_ATT_HEADS * ATT_HEAD_DIM
    d_mqk = N_MLSTM_HEADS * MLSTM_QK_DIM
    d_mv = N_MLSTM_HEADS * MLSTM_V_DIM
    n_main = prm["w_main"].shape[1]
    grid = (nb, lp // tm)
    row = lambda w: pl.BlockSpec((1, tm, w), lambda b, i: (b, i, 0))
    const = lambda r, c: pl.BlockSpec((r, c), lambda b, i: (0, 0), pipeline_mode=pl.Buffered(1))
    out_shape = (
        jax.ShapeDtypeStruct((nb, lp, d), F32),
        jax.ShapeDtypeStruct((nb, lp, d_att), BF16),
        jax.ShapeDtypeStruct((nb, valid_len, N_ATT_HEADS, ATT_HEAD_DIM), F32),
        jax.ShapeDtypeStruct((nb, lp, d_att), BF16),
        jax.ShapeDtypeStruct((nb, valid_len, N_ATT_HEADS, ATT_HEAD_DIM), F32),
        jax.ShapeDtypeStruct((nb, lp, d_att), BF16),
        jax.ShapeDtypeStruct((nb, lp, LANES), F32),
        jax.ShapeDtypeStruct((nb, lp, d_mqk), BF16),
        jax.ShapeDtypeStruct((nb, lp, d_mqk), BF16),
        jax.ShapeDtypeStruct((nb, lp, d_mv), BF16),
        jax.ShapeDtypeStruct((nb, lp, d_mv), F32),
    )
    heads = pl.BlockSpec((1, tm, N_ATT_HEADS, ATT_HEAD_DIM), lambda b, i: (b, i, 0, 0))
    out_specs = (row(d), row(d_att), heads, row(d_att), heads, row(d_att), row(LANES),
                 row(d_mqk), row(d_mqk), row(d_mv), row(d_mv))
    return pl.pallas_call(
        functools.partial(_proj_kernel, tm=tm, valid_len=valid_len),
        grid=grid,
        in_specs=[row(d), const(1, d), const(1, d), const(d, n_main), const(d, LANES), const(d, LANES),
                  const(1, LANES), const(1, ATT_HEAD_DIM), const(1, ATT_HEAD_DIM)],
        out_specs=out_specs,
        out_shape=out_shape,
        compiler_params=_cparams("parallel", "arbitrary"),
        name="proj",
    )(x, prm["ln_in_g"], prm["ln_in_b"], prm["w_main"], prm["wg_hi"], prm["wg_lo"], prm["gate_bias"],
      prm["g_att_q"], prm["g_att_k"])


def _gate_scan_kernel(g_ref, cl_ref, clt_ref, gt_ref, ka_ref, qa_ref, *, n_chunks):
    r = lax.broadcasted_iota(I32, (CHUNK, CHUNK), 0)
    c = lax.broadcasted_iota(I32, (CHUNK, CHUNK), 1)
    tri = jnp.where(c <= r, 1.0, 0.0).astype(BF16)
    carry = jnp.zeros((1, LANES), F32)
    for ci in range(n_chunks):
        sl = slice(ci * CHUNK, (ci + 1) * CHUNK)
        g = g_ref[0, sl, :]
        hi, mid, lo = _split3(g)
        loc = _dot(tri, hi) + (_dot(tri, mid) + _dot(tri, lo))
        glob = loc + carry
        carry = glob[CHUNK - 1:CHUNK, :]
        cl_ref[0, sl, :] = loc
        clt_ref[0, ci] = loc.T
        gt_ref[0, ci] = g.T
        for hh in range(N_ATT_HEADS):
            cum = glob[:, GATE_LF_ATT + hh:GATE_LF_ATT + hh + 1]
            t0 = cum.astype(BF16).astype(F32)
            r1 = cum - t0
            t1 = r1.astype(BF16).astype(F32)
            t2 = (r1 - t1).astype(BF16).astype(F32)
            hs = slice(hh * ATT_HEAD_DIM, (hh + 1) * ATT_HEAD_DIM)
            ones = jnp.where(c < 3, 1.0, 0.0)
            terms = jnp.where(c == 3, t0, jnp.where(c == 4, t1, jnp.where(c == 5, t2, 0.0)))
            qa_ref[0, sl, hs] = (ones + terms).astype(BF16)
            ones_k = jnp.where(jnp.logical_and(c >= 3, c < 6), 1.0, 0.0)
            terms_k = jnp.where(c == 0, t0, jnp.where(c == 1, t1, jnp.where(c == 2, t2, 0.0)))
            ka_ref[0, sl, hs] = (ones_k - terms_k).astype(BF16)


def _gate_scan(gates):
    nb, lp, _ = gates.shape
    nc = lp // CHUNK
    d_att = N_ATT_HEADS * ATT_HEAD_DIM
    col = pl.BlockSpec((1, lp, LANES), lambda b: (b, 0, 0))
    rowt = pl.BlockSpec((1, nc, LANES, CHUNK), lambda b: (b, 0, 0, 0))
    aug = pl.BlockSpec((1, lp, d_att), lambda b: (b, 0, 0))
    return pl.pallas_call(
        functools.partial(_gate_scan_kernel, n_chunks=nc),
        grid=(nb,),
        in_specs=[col],
        out_specs=(col, rowt, rowt, aug, aug),
        out_shape=(jax.ShapeDtypeStruct((nb, lp, LANES), F32),
                   jax.ShapeDtypeStruct((nb, nc, LANES, CHUNK), F32),
                   jax.ShapeDtypeStruct((nb, nc, LANES, CHUNK), F32),
                   jax.ShapeDtypeStruct((nb, lp, d_att), BF16),
                   jax.ShapeDtypeStruct((nb, lp, d_att), BF16)),
        compiler_params=_cparams("parallel"),
        name="gate_scan",
    )(gates)


def _fox_kernel(q_ref, qa_ref, k_ref, ka_ref, v_ref, o_ref, kx_sc, vt_sc, qx_sc, m_sc, l_sc, acc_sc,
                *, n_chunks, tq):
    ti = pl.program_id(1)
    dh = ATT_HEAD_DIM

    @pl.when(ti == 0)
    def _():
        for j in range(n_chunks):
            rows = slice(j * CHUNK, (j + 1) * CHUNK)
            for hh in range(N_ATT_HEADS):
                sl = slice(hh * dh, (hh + 1) * dh)
                kx_sc[j, :, 2 * hh * dh:(2 * hh + 1) * dh] = k_ref[0, rows, sl]
                kx_sc[j, :, (2 * hh + 1) * dh:(2 * hh + 2) * dh] = ka_ref[0, rows, sl]
                vt_sc[hh, j] = v_ref[0, rows, sl].astype(F32).T.astype(BF16)

    for hh in range(N_ATT_HEADS):
        sl = slice(hh * dh, (hh + 1) * dh)
        qx_sc[:, 2 * hh * dh:(2 * hh + 1) * dh] = q_ref[0, :, sl]
        qx_sc[:, (2 * hh + 1) * dh:(2 * hh + 2) * dh] = qa_ref[0, :, sl]
    m_sc[...] = jnp.full(m_sc.shape, NEG, F32)
    l_sc[...] = jnp.zeros(l_sc.shape, F32)
    acc_sc[...] = jnp.zeros(acc_sc.shape, F32)
    key = lax.broadcasted_iota(I32, (CHUNK, tq), 0)
    qry = lax.broadcasted_iota(I32, (CHUNK, tq), 1)

    def step(j, masked):
        scores = []
        for hh in range(N_ATT_HEADS):
            xs = slice(2 * hh * dh, (2 * hh + 2) * dh)
            scores.append(_dot_nt(kx_sc[j, :, xs], qx_sc[:, xs]))
        probs, alphas = [], []
        for hh in range(N_ATT_HEADS):
            s = scores[hh]
            if masked:
                s = jnp.where(j * CHUNK + key <= ti * tq + qry, s, NEG)
            m_old = m_sc[hh]
            m_new = jnp.maximum(m_old, jnp.max(s, axis=0, keepdims=True))
            p = jnp.exp(s - m_new)
            alpha = jnp.exp(m_old - m_new)
            l_sc[hh] = alpha * l_sc[hh] + jnp.sum(p, axis=0, keepdims=True)
            m_sc[hh] = m_new
            probs.append(p.astype(BF16))
            alphas.append(alpha)
        for hh in range(N_ATT_HEADS):
            acc_sc[hh] = alphas[hh] * acc_sc[hh] + _dot(vt_sc[hh, j], probs[hh])

    def body(j, carry):
        step(j, False)
        return carry

    per = tq // CHUNK
    lax.fori_loop(0, ti * per, body, 0)
    for dj in range(per):
        j = ti * per + dj

        @pl.when(j < n_chunks)
        def _():
            step(j, True)

    for hh in range(N_ATT_HEADS):
        sl = slice(hh * dh, (hh + 1) * dh)
        o_ref[0, :, sl] = (acc_sc[hh] / l_sc[hh]).T.astype(o_ref.dtype)


def _fox_prompt(q, q_aug, kb, k_aug, vb):
    nb, lp, d_att = q.shape
    nc = lp // CHUNK
    tq = FOX_TQ
    blk = pl.BlockSpec((1, tq, d_att), lambda b, i: (b, i, 0))
    full = pl.BlockSpec((1, lp, d_att), lambda b, i: (b, 0, 0))
    nh, dh = N_ATT_HEADS, ATT_HEAD_DIM
    return pl.pallas_call(
        functools.partial(_fox_kernel, n_chunks=nc, tq=tq),
        grid=(nb, pl.cdiv(lp, tq)),
        in_specs=[blk, blk, full, full, full],
        out_specs=blk,
        out_shape=jax.ShapeDtypeStruct((nb, lp, d_att), BF16),
        scratch_shapes=[pltpu.VMEM((nc, CHUNK, 2 * d_att), BF16),
                        pltpu.VMEM((nh, nc, dh, CHUNK), BF16),
                        pltpu.VMEM((tq, 2 * d_att), BF16),
                        pltpu.VMEM((nh, 1, tq), F32), pltpu.VMEM((nh, 1, tq), F32),
                        pltpu.VMEM((nh, dh, tq), F32)],
        compiler_params=_cparams("parallel", "arbitrary"),
        name="fox_prompt",
    )(q, q_aug, kb, k_aug, vb)


def _mlstm_kernel(q_ref, k_ref, v_ref, g_ref, cl_ref, gt_ref, clt_ref,
                  h_ref, c_out_ref, n_out_ref, m_out_ref, c_sc, n_sc, m_sc):
    ci = pl.program_id(1)

    @pl.when(ci == 0)
    def _():
        c_sc[...] = jnp.zeros(c_sc.shape, F32)
        n_sc[...] = jnp.zeros(n_sc.shape, F32)
        m_sc[...] = jnp.zeros(m_sc.shape, F32)

    r = lax.broadcasted_iota(I32, (CHUNK, CHUNK), 0)
    c = lax.broadcasted_iota(I32, (CHUNK, CHUNK), 1)
    causal = c <= r
    last = slice(CHUNK - 1, CHUNK)
    heads = range(N_MLSTM_HEADS)
    qk = [slice(hh * MLSTM_QK_DIM, (hh + 1) * MLSTM_QK_DIM) for hh in heads]
    vv = [slice(hh * MLSTM_V_DIM, (hh + 1) * MLSTM_V_DIM) for hh in heads]
    qh = [q_ref[0, :, qk[hh]] for hh in heads]
    kh = [k_ref[0, :, qk[hh]] for hh in heads]
    vh = [v_ref[0, :, vv[hh]] for hh in heads]
    c0 = [c_sc[hh] for hh in heads]
    qk_raw = [_dot_nt(qh[hh], kh[hh]) for hh in heads]
    q_c0 = [_dot(qh[hh], c0[hh].astype(BF16)) for hh in heads]

    s_bf, kw_bf, stash = [], [], []
    for hh in heads:
        ig_l, lf_l = GATE_IG + hh, GATE_LF_M + hh
        b_col = cl_ref[0, :, lf_l:lf_l + 1]
        ig_col = g_ref[0, :, ig_l:ig_l + 1]
        b_row = clt_ref[0, 0, lf_l:lf_l + 1, :]
        ig_row = gt_ref[0, 0, ig_l:ig_l + 1, :]
        m0 = m_sc[hh][:, 0:1]
        n0 = n_sc[hh]
        dmat = jnp.where(causal, b_col - b_row + ig_row, NEG)
        inter = b_col + m0
        mt = jnp.maximum(inter, jnp.max(dmat, axis=-1, keepdims=True))
        w_inter = jnp.exp(inter - mt)
        s = qk_raw[hh] * jnp.exp(dmat - mt)
        den = jnp.sum(s, axis=-1, keepdims=True) + w_inter * jnp.sum(qh[hh].astype(F32) * n0, axis=-1, keepdims=True)
        m_new = mt[last, :]
        b_last = b_col[last, :]
        w_state = jnp.exp(b_last - b_col + ig_col - m_new)
        decay = jnp.exp(b_last + m0 - m_new)
        kw = kh[hh].astype(F32) * w_state
        s_bf.append(s.astype(BF16))
        kw_bf.append(kw.astype(BF16))
        stash.append((w_inter, den, mt, m_new, decay, decay * n0 + jnp.sum(kw, axis=0, keepdims=True)))

    s_v = [_dot(s_bf[hh], vh[hh]) for hh in heads]
    kw_v = [_dot_tn(kw_bf[hh], vh[hh]) for hh in heads]
    for hh in heads:
        w_inter, den, mt, m_new, decay, n_new = stash[hh]
        num = s_v[hh] + w_inter * q_c0[hh]
        h_ref[0, :, vv[hh]] = num / jnp.maximum(jnp.abs(den), jnp.exp(-mt))
        c_new = decay * c0[hh] + kw_v[hh]
        c_sc[hh] = c_new
        n_sc[hh] = n_new
        m_sc[hh] = jnp.broadcast_to(m_new, (1, LANES))
        c_out_ref[0, hh] = c_new
        n_out_ref[0, hh] = n_new
        m_out_ref[0, hh] = jnp.broadcast_to(m_new, (1, LANES))


def _mlstm_prompt(qm, km, vm, gates, cum_l, gates_t, cum_lt):
    nb, lp, d_mqk = qm.shape
    d_mv = vm.shape[-1]
    nc = lp // CHUNK
    hm = N_MLSTM_HEADS
    blk = lambda w: pl.BlockSpec((1, CHUNK, w), lambda b, i: (b, i, 0))
    tblk = pl.BlockSpec((1, 1, LANES, CHUNK), lambda b, i: (b, i, 0, 0))
    return pl.pallas_call(
        _mlstm_kernel,
        grid=(nb, nc),
        in_specs=[blk(d_mqk), blk(d_mqk), blk(d_mv), blk(LANES), blk(LANES), tblk, tblk],
        out_specs=(blk(d_mv),
                   pl.BlockSpec((1, hm, MLSTM_QK_DIM, MLSTM_V_DIM), lambda b, i: (b, 0, 0, 0)),
                   pl.BlockSpec((1, hm, 1, MLSTM_QK_DIM), lambda b, i: (b, 0, 0, 0)),
                   pl.BlockSpec((1, hm, 1, LANES), lambda b, i: (b, 0, 0, 0))),
        out_shape=(jax.ShapeDtypeStruct((nb, lp, d_mv), F32),
                   jax.ShapeDtypeStruct((nb, hm, MLSTM_QK_DIM, MLSTM_V_DIM), F32),
                   jax.ShapeDtypeStruct((nb, hm, 1, MLSTM_QK_DIM), F32),
                   jax.ShapeDtypeStruct((nb, hm, 1, LANES), F32)),
        scratch_shapes=[pltpu.VMEM((hm, MLSTM_QK_DIM, MLSTM_V_DIM), F32),
                        pltpu.VMEM((hm, 1, MLSTM_QK_DIM), F32),
                        pltpu.VMEM((hm, 1, LANES), F32)],
        compiler_params=_cparams("parallel", "arbitrary"),
        name="mlstm_prompt",
    )(qm, km, vm, gates, cum_l, gates_t, cum_lt)


def _paged_kernel(pt_ref, q_ref, kn_ref, vn_ref, lfn_ref, *refs, pps):
    k_refs = refs[:pps]
    v_refs = refs[pps:2 * pps]
    lf_refs = refs[2 * pps:3 * pps]
    o_ref, lf_sc, bias_sc, m_sc, l_sc, acc_sc, carry_sc = refs[3 * pps:]
    g = pl.program_id(1)
    nh, dh = N_ATT_HEADS, ATT_HEAD_DIM
    page = k_refs[0].shape[0]
    width = page * nh
    q8 = q_ref[0]

    @pl.when(g == 0)
    def _():
        m_sc[...] = jnp.sum(q8 * kn_ref[0], axis=-1, keepdims=True)
        l_sc[...] = jnp.ones(l_sc.shape, F32)
        acc_sc[...] = vn_ref[0]
        carry_sc[...] = jnp.zeros(carry_sc.shape, F32)

    for u in range(pps):
        lf_sc[u:u + 1, :] = lf_refs[u][...]
    lf = lf_sc[...]
    lane = lax.broadcasted_iota(I32, (pps, width), 1)
    incl = lf
    k = nh
    while k < width:
        incl = incl + jnp.where(lane < width - k, pltpu.roll(incl, width - k, axis=1), 0.0)
        k *= 2
    tot = jnp.where(lane < nh, incl, 0.0)
    k = nh
    while k < width:
        tot = tot + pltpu.roll(tot, k, axis=1)
        k *= 2
    run = carry_sc[...]
    for u in range(pps):
        bias_sc[u:u + 1, :] = run
        run = run + tot[u:u + 1, :]
    carry_sc[...] = run
    bias = bias_sc[...] + (incl - lf) + lfn_ref[0]

    own = lax.broadcasted_iota(I32, (nh, width), 1) % nh == lax.broadcasted_iota(I32, (nh, width), 0)
    q_bf = q8.astype(BF16)
    scores = []
    m_old = m_sc[...]
    m_new = m_old
    for u in range(pps):
        kf = k_refs[u][...].reshape(width, dh).astype(BF16)
        s = jnp.where(own, _dot_nt(q_bf, kf) + bias[u:u + 1, :], NEG)
        m_new = jnp.maximum(m_new, jnp.max(s, axis=-1, keepdims=True))
        scores.append(s)
    alpha = jnp.exp(m_old - m_new)
    l_new = alpha * l_sc[...]
    acc = alpha * acc_sc[...]
    for u in range(pps):
        p = jnp.exp(scores[u] - m_new)
        l_new = l_new + jnp.sum(p, axis=-1, keepdims=True)
        acc = acc + _dot(p.astype(BF16), v_refs[u][...].reshape(width, dh).astype(BF16))
    m_sc[...] = m_new
    l_sc[...] = l_new
    acc_sc[...] = acc

    @pl.when(g == pl.num_programs(1) - 1)
    def _():
        o_ref[0] = acc / l_new


def _paged_attention(q, k_new, v_new, lf_new_flat, cache_k, cache_v, cache_lf_flat, page_table):
    db, n_pages = page_table.shape
    _, n_pool, page, nh, dh = cache_k.shape
    width = page * nh
    pps = PAGES_PER_STEP
    steps = n_pages // pps
    tok = lambda: pl.BlockSpec((1, nh, dh), lambda b, g, pt: (b, 0, 0))

    def page_id(b, g, pt, u):
        return pt[b * n_pages + (n_pages - 1 - (g * pps + u))]

    kv_specs = [pl.BlockSpec((None, None, page, nh, dh),
                             functools.partial(lambda b, g, pt, u: (0, page_id(b, g, pt, u), 0, 0, 0), u=u))
                for u in range(pps)]
    lf_specs = [pl.BlockSpec((None, 1, width),
                             functools.partial(lambda b, g, pt, u: (page_id(b, g, pt, u), 0, 0), u=u))
                for u in range(pps)]
    grid_spec = pltpu.PrefetchScalarGridSpec(
        num_scalar_prefetch=1,
        grid=(db, steps),
        in_specs=[tok(), tok(), tok(), pl.BlockSpec((1, 1, width), lambda b, g, pt: (b, 0, 0))]
        + kv_specs + kv_specs + lf_specs,
        out_specs=tok(),
        scratch_shapes=[pltpu.VMEM((pps, width), F32), pltpu.VMEM((pps, width), F32),
                        pltpu.VMEM((nh, 1), F32), pltpu.VMEM((nh, 1), F32),
                        pltpu.VMEM((nh, dh), F32), pltpu.VMEM((1, width), F32)],
    )
    return pl.pallas_call(
        functools.partial(_paged_kernel, pps=pps),
        grid_spec=grid_spec,
        out_shape=jax.ShapeDtypeStruct((db, nh, dh), F32),
        compiler_params=_cparams("parallel", "arbitrary"),
        name="paged_fox",
    )(page_table.reshape(-1), q, k_new, v_new, lf_new_flat, *([cache_k] * pps), *([cache_v] * pps),
      *([cache_lf_flat] * pps))


def _mstep_kernel(q_ref, k_ref, v_ref, ig_ref, lf_ref, c_ref, n_ref, m_ref,
                  h_ref, c_out_ref, n_out_ref, m_out_ref):
    for hh in range(N_MLSTM_HEADS):
        q = q_ref[0, hh].astype(F32)
        k = k_ref[0, hh].astype(F32)
        v = v_ref[0, hh].astype(F32)
        ig = ig_ref[0, hh]
        lf = lf_ref[0, hh]
        c0 = c_ref[0, hh]
        n0 = n_ref[0, hh]
        m0 = m_ref[0, hh]
        inter = lf + m0
        m = jnp.maximum(inter, ig)
        w_intra = jnp.exp(ig - m)
        w_inter = jnp.exp(inter - m)
        s = jnp.sum(q * k, axis=0, keepdims=True) * w_intra
        num = s * v + w_inter * jnp.sum(q * c0, axis=0, keepdims=True)
        den = s + w_inter * jnp.sum(q * n0, axis=0, keepdims=True)
        h_ref[0, hh] = num / jnp.maximum(jnp.abs(den), jnp.exp(-m))
        c_out_ref[0, hh] = w_inter * c0 + w_intra * (k * v)
        n_out_ref[0, hh] = w_inter * n0 + w_intra * k
        m_out_ref[0, hh] = m


def _mlstm_step(q_col, k_col, v_row, ig, lf, state_c, state_n_col, state_m):
    db, hm, dk, dv = state_c.shape
    spec = lambda a, b_: pl.BlockSpec((1, hm, a, b_), lambda i: (i, 0, 0, 0))
    return pl.pallas_call(
        _mstep_kernel,
        grid=(db,),
        in_specs=[spec(dk, 1), spec(dk, 1), spec(1, dv), spec(1, 1), spec(1, 1), spec(dk, dv), spec(dk, 1), spec(1, 1)],
        out_specs=(spec(1, dv), spec(dk, dv), spec(dk, 1), spec(1, 1)),
        out_shape=(jax.ShapeDtypeStruct((db, hm, 1, dv), F32), jax.ShapeDtypeStruct((db, hm, dk, dv), F32),
                   jax.ShapeDtypeStruct((db, hm, dk, 1), F32), jax.ShapeDtypeStruct((db, hm, 1, 1), F32)),
        compiler_params=_cparams("parallel"),
        name="mlstm_step",
    )(q_col, k_col, v_row, ig, lf, state_c, state_n_col, state_m)


def _mix_kernel(att_ref, hm_ref, og_ref, h_ref, gm_ref, w_ref, lg_ref, lb_ref, wrh_ref, wrl_ref, br_ref,
                h1_ref, h1p_ref, e_ref, gate_ref, *, alpha):
    d_att = att_ref.shape[-1]
    parts = []
    for hh in range(N_MLSTM_HEADS):
        vv = slice(hh * MLSTM_V_DIM, (hh + 1) * MLSTM_V_DIM)
        x = hm_ref[:, vv]
        xn = x * lax.rsqrt(jnp.mean(x * x, axis=-1, keepdims=True) + RMS_EPS) * gm_ref[:, vv]
        parts.append((og_ref[:, vv] * xn).astype(BF16))
    mix = _dot(att_ref[...].astype(BF16), w_ref[0:d_att, :])
    for hh in range(N_MLSTM_HEADS):
        r0 = d_att + hh * MLSTM_V_DIM
        mix = mix + _dot(parts[hh], w_ref[r0:r0 + MLSTM_V_DIM, :])
    h1 = _layer_norm(alpha * h_ref[...] + mix, lg_ref[...], lb_ref[...])
    h1_ref[...] = h1
    half = h1.shape[-1] // 2
    hi = lax.bitcast_convert_type(h1[:, :half].astype(BF16).astype(F32), jnp.uint32)
    lo = lax.bitcast_convert_type(h1[:, half:].astype(BF16).astype(F32), jnp.uint32)
    h1p_ref[...] = hi | (lo >> 16)

    logits = _dot3(h1, wrh_ref[...], wrl_ref[...]) + br_ref[...]
    lane = lax.broadcasted_iota(I32, logits.shape, 1).astype(F32)
    e_out = jnp.zeros(logits.shape, F32)
    top = []
    for kk in range(TOP_K):
        mx = jnp.max(logits, axis=-1, keepdims=True)
        idx = jnp.min(jnp.where(logits == mx, lane, float(LANES)), axis=-1, keepdims=True)
        logits = jnp.where(lane == idx, -jnp.inf, logits)
        e_out = jnp.where(lane == float(kk), idx, e_out)
        top.append(mx)
    ex = [jnp.exp(t - top[0]) for t in top]
    inv = 1.0 / (ex[0] + ex[1] + ex[2] + ex[3])
    gate = jnp.zeros(logits.shape, F32)
    for kk in range(TOP_K):
        gate = jnp.where(lane == float(kk), ex[kk] * inv, gate)
    e_ref[...] = e_out.astype(I32)
    gate_ref[...] = gate


def _mix(att, hm, og, h, prm, tm, alpha):
    rows, d = h.shape
    d_att = att.shape[-1]
    d_mv = hm.shape[-1]
    row = lambda w: pl.BlockSpec((tm, w), lambda i: (i, 0))
    const = lambda r, c: pl.BlockSpec((r, c), lambda i: (0, 0), pipeline_mode=pl.Buffered(1))
    return pl.pallas_call(
        functools.partial(_mix_kernel, alpha=alpha),
        grid=(rows // tm,),
        in_specs=[row(d_att), row(d_mv), row(d_mv), row(d), const(1, d_mv), const(d_att + d_mv, d),
                  const(1, d), const(1, d), const(d, LANES), const(d, LANES), const(1, LANES)],
        out_specs=(row(d), row(d // 2), row(LANES), row(LANES)),
        out_shape=(jax.ShapeDtypeStruct((rows, d), F32), jax.ShapeDtypeStruct((rows, d // 2), jnp.uint32),
                   jax.ShapeDtypeStruct((rows, LANES), I32), jax.ShapeDtypeStruct((rows, LANES), F32)),
        compiler_params=_cparams("parallel"),
        name="mix",
    )(att, hm, og, h, prm["g_m_out"], prm["w_out"], prm["ln1_g"], prm["ln1_b"], prm["wr_hi"], prm["wr_lo"],
      prm["b_router"])


def _moe_kernel(ie_ref, ns_ref, iv_ref, st_ref, sp_ref,
                x_hbm, wg_ref, wu_ref, bg_ref, bu_ref, wd_ref, bd_ref,
                y_hbm, xbuf, xb, acc, gsem, ssem, *, n_tok, n_items, n_ft):
    i = pl.program_id(0)
    f = pl.program_id(1)
    nf = pl.num_programs(1)
    slot = i % 2
    other = 1 - slot
    n_sub = ns_ref[i]
    valid = iv_ref[i] == 1
    half = xbuf.shape[-1]
    dump_base = TOP_K * n_tok

    def gather_rows(item, sl, row0, count):
        base = st_ref[item] + row0
        dst = xbuf.at[sl, pl.ds(row0, count), :]
        for r in range(count):
            pair = sp_ref[base + r]
            tok = jnp.maximum(pair, 0) >> LOG2_TOP_K
            pltpu.make_async_copy(x_hbm.at[pl.ds(tok, 1), :], dst.at[pl.ds(r, 1), :], gsem.at[sl]).start()

    def scatter_rows(item, sl, row0, count):
        base = st_ref[item] + row0
        n_real = ns_ref[item] * MOE_SUB - row0
        dump = dump_base + sl * MOE_ROWS + row0
        src = acc.at[sl, pl.ds(row0, count), :]
        for r in range(count):
            pair = sp_ref[base + r]
            real = jnp.logical_and(pair >= 0, r < n_real)
            dst = jnp.where(real, (pair & (TOP_K - 1)) * n_tok + (pair >> LOG2_TOP_K), dump + r)
            pltpu.make_async_copy(src.at[pl.ds(r, 1), :], y_hbm.at[pl.ds(dst, 1), :], ssem.at[sl]).start()

    def gather_wait(sl):
        pltpu.make_async_copy(x_hbm.at[pl.ds(0, MOE_ROWS), :], xbuf.at[sl], gsem.at[sl]).wait()

    def scatter_wait(sl):
        pltpu.make_async_copy(acc.at[sl], y_hbm.at[pl.ds(0, MOE_ROWS), :], ssem.at[sl]).wait()

    def all_steps(fn):
        def body(step, carry):
            fn(pl.multiple_of(step * MOE_SUB, MOE_SUB))
            return carry
        lax.fori_loop(0, MOE_ROWS // MOE_SUB, body, 0)

    nxt = jnp.minimum(i + 1, n_items - 1)
    has_next = jnp.logical_and(i + 1 < n_items, iv_ref[nxt] == 1)
    prv = jnp.maximum(i - 1, 0)
    has_prev = jnp.logical_and(i >= 1, iv_ref[prv] == 1)
    prv2 = jnp.maximum(i - 2, 0)
    has_prev2 = jnp.logical_and(i >= 2, iv_ref[prv2] == 1)

    @pl.when(f == 0)
    def _():
        @pl.when(jnp.logical_and(i == 0, valid))
        def _():
            all_steps(lambda row0: gather_rows(0, 0, row0, MOE_SUB))

        @pl.when(valid)
        def _():
            gather_wait(slot)
            packed = xbuf[slot]
            hi = lax.bitcast_convert_type(packed & jnp.uint32(0xFFFF0000), F32)
            lo = lax.bitcast_convert_type(packed << 16, F32)
            xb[:, 0:half] = hi.astype(BF16)
            xb[:, half:2 * half] = lo.astype(BF16)

        @pl.when(has_prev2)
        def _():
            scatter_wait(slot)

        @pl.when(valid)
        def _():
            acc[slot] = jnp.broadcast_to(bd_ref[0], acc.shape[1:])

    per_step = MOE_ROWS // n_ft
    row0 = pl.multiple_of(f * per_step, per_step)
    per_block = MOE_BLOCK // MOE_SUB
    n_full = n_sub // per_block
    max_full = MOE_ROWS // MOE_BLOCK
    share = per_step // max_full
    fused = jnp.logical_and(jnp.logical_and(valid, n_full == max_full), jnp.logical_and(has_next, has_prev))

    @pl.when(jnp.logical_and(has_next, jnp.logical_not(fused)))
    def _():
        gather_rows(i + 1, other, row0, per_step)

    @pl.when(jnp.logical_and(has_prev, jnp.logical_not(fused)))
    def _():
        scatter_rows(i - 1, other, row0, per_step)

    @pl.when(valid)
    def _():
        wg = wg_ref[0].astype(BF16)
        wu = wu_ref[0].astype(BF16)
        wd = wd_ref[0].astype(BF16)

        def rows_block(start, count):
            rows = pl.ds(start, count)
            xs = xb[rows, :]
            gp = _dot(xs, wg) + bg_ref[0]
            up = _dot(xs, wu) + bu_ref[0]
            gp = jnp.minimum(gp, SWIGLU_LIMIT)
            up = jnp.clip(up, -SWIGLU_LIMIT, SWIGLU_LIMIT)
            act = gp * _sigmoid(SWIGLU_ALPHA * gp) * (up + 1.0)
            acc[slot, rows, :] += _dot(act.astype(BF16), wd)

        def body(blk, carry):
            rows_block(pl.multiple_of(blk * MOE_BLOCK, MOE_BLOCK), MOE_BLOCK)
            return carry

        def body_fused(blk, carry):
            r0 = pl.multiple_of(row0 + blk * share, 8)
            gather_rows(i + 1, other, r0, share)
            scatter_rows(i - 1, other, r0, share)
            rows_block(pl.multiple_of(blk * MOE_BLOCK, MOE_BLOCK), MOE_BLOCK)
            return carry

        @pl.when(fused)
        def _():
            lax.fori_loop(0, max_full, body_fused, 0)

        @pl.when(jnp.logical_not(fused))
        def _():
            lax.fori_loop(0, n_full, body, 0)

        tail = n_sub - n_full * per_block
        for nn in range(1, per_block):
            @pl.when(tail == nn)
            def _():
                rows_block(pl.multiple_of(n_full * MOE_BLOCK, MOE_BLOCK), nn * MOE_SUB)

    @pl.when(jnp.logical_and(i == n_items - 1, f == nf - 1))
    def _():
        @pl.when(valid)
        def _():
            all_steps(lambda row0: scatter_rows(i, slot, row0, MOE_SUB))
            scatter_wait(slot)

        @pl.when(has_prev)
        def _():
            scatter_wait(other)


def _moe(x_packed, item_e, item_ns, item_valid, item_start, slot_pair, w_gate_up, b_gate_up, w_down, b_down):
    n_tok, half = x_packed.shape
    d = 2 * half
    n_exp, _, two_f = w_gate_up.shape
    nf = two_f // 2 // MOE_FT
    n_items = item_e.shape[0]
    assert MOE_ROWS % nf == 0 and (MOE_ROWS // nf) % (8 * (MOE_ROWS // MOE_BLOCK)) == 0

    def ftile(i, f, iv):
        return f * iv[i] + (nf - 1) * (1 - iv[i])

    grid_spec = pltpu.PrefetchScalarGridSpec(
        num_scalar_prefetch=5,
        grid=(n_items, nf),
        in_specs=[
            pl.BlockSpec(memory_space=pl.ANY),
            pl.BlockSpec((1, d, MOE_FT), lambda i, f, ie, ns, iv, st, sp: (ie[i], 0, ftile(i, f, iv))),
            pl.BlockSpec((1, d, MOE_FT), lambda i, f, ie, ns, iv, st, sp: (ie[i], 0, nf + ftile(i, f, iv))),
            pl.BlockSpec((1, 1, MOE_FT), lambda i, f, ie, ns, iv, st, sp: (ie[i], 0, ftile(i, f, iv))),
            pl.BlockSpec((1, 1, MOE_FT), lambda i, f, ie, ns, iv, st, sp: (ie[i], 0, nf + ftile(i, f, iv))),
            pl.BlockSpec((1, MOE_FT, d), lambda i, f, ie, ns, iv, st, sp: (ie[i], ftile(i, f, iv), 0)),
            pl.BlockSpec((1, 1, d), lambda i, f, ie, ns, iv, st, sp: (ie[i], 0, 0)),
        ],
        out_specs=pl.BlockSpec(memory_space=pl.ANY),
        scratch_shapes=[pltpu.VMEM((2, MOE_ROWS, half), jnp.uint32), pltpu.VMEM((MOE_ROWS, d), BF16),
                        pltpu.VMEM((2, MOE_ROWS, d), F32),
                        pltpu.SemaphoreType.DMA((2,)), pltpu.SemaphoreType.DMA((2,))],
    )
    return pl.pallas_call(
        functools.partial(_moe_kernel, n_tok=n_tok, n_items=n_items, n_ft=nf),
        grid_spec=grid_spec,
        out_shape=jax.ShapeDtypeStruct((TOP_K * n_tok + 2 * MOE_ROWS, d), F32),
        compiler_params=_cparams("arbitrary", "arbitrary"),
        name="moe",
    )(item_e, item_ns, item_valid, item_start, slot_pair, x_packed, w_gate_up, w_gate_up,
      b_gate_up.reshape(n_exp, 1, two_f), b_gate_up.reshape(n_exp, 1, two_f), w_down, b_down.reshape(n_exp, 1, d))


def _combine_kernel(y0_ref, y1_ref, y2_ref, y3_ref, gate_ref, h1_ref, lg_ref, lb_ref, o_ref, *, alpha):
    acc = alpha * h1_ref[...]
    for kk, y_ref in enumerate((y0_ref, y1_ref, y2_ref, y3_ref)):
        acc = acc + y_ref[...] * gate_ref[:, kk:kk + 1]
    o_ref[0] = _layer_norm(acc, lg_ref[...], lb_ref[...])


def _combine(y_pairs, gate, h1, ln_g, ln_b, alpha, *, n_tok, tok0, group_rows, skip, out_rows, tm):
    d = h1.shape[-1]
    groups = h1.shape[0] // group_rows
    assert out_rows % tm == 0 and skip % 8 == 0 and group_rows % 8 == 0 and tok0 % 8 == 0 and n_tok % 8 == 0

    def rows(base):
        return lambda b, j: (pl.multiple_of(base + b * group_rows + skip + j * tm, 8), 0)

    def el(w, base):
        return pl.BlockSpec((pl.Element(tm), pl.Element(w)), rows(base))

    const = lambda c: pl.BlockSpec((1, c), lambda b, j: (0, 0))
    y_specs = [el(d, kk * n_tok + tok0) for kk in range(TOP_K)]
    return pl.pallas_call(
        functools.partial(_combine_kernel, alpha=alpha),
        grid=(groups, out_rows // tm),
        in_specs=y_specs + [el(LANES, 0), el(d, 0), const(d), const(d)],
        out_specs=pl.BlockSpec((1, tm, d), lambda b, j: (b, j, 0)),
        out_shape=jax.ShapeDtypeStruct((groups, out_rows, d), F32),
        compiler_params=_cparams("parallel", "parallel"),
        name="combine",
    )(*([y_pairs] * TOP_K), gate, h1, ln_g, ln_b)


def _route(top_e):
    n_pairs = top_e.shape[0] * TOP_K
    per_item = MOE_ROWS // MOE_SUB
    pair_e = top_e.reshape(-1)
    order = jnp.argsort(pair_e, stable=True).astype(I32)
    e_sorted = pair_e[order]
    edges = jnp.arange(N_EXPERTS + 1, dtype=I32)
    bounds = jnp.sum((e_sorted[None, :] < edges[:, None]).astype(I32), axis=1)
    start = bounds[:-1]
    counts = bounds[1:] - start
    n_sb = (counts + MOE_SUB - 1) // MOE_SUB
    sb_end = jnp.cumsum(n_sb)
    sb_start = sb_end - n_sb
    max_sb = n_pairs // MOE_SUB + N_EXPERTS
    slot = jnp.arange(max_sb * MOE_SUB, dtype=I32)
    e_slot = jnp.minimum(jnp.sum((slot[:, None] >= (sb_end * MOE_SUB)[None, :]).astype(I32), axis=1), N_EXPERTS - 1)
    rank = slot - sb_start[e_slot] * MOE_SUB
    src = jnp.clip(start[e_slot] + rank, 0, n_pairs - 1)
    slot_pair = jnp.where(rank < counts[e_slot], order[src], -1).astype(I32)
    slot_pair = jnp.concatenate([slot_pair, jnp.full((MOE_ROWS,), -1, I32)])

    n_it = (n_sb + per_item - 1) // per_item
    it_end = jnp.cumsum(n_it)
    it_start = it_end - n_it
    max_items = (max_sb + (per_item - 1) * N_EXPERTS) // per_item
    idx = jnp.arange(max_items, dtype=I32)
    n_valid = it_end[-1]
    item_valid = (idx < n_valid).astype(I32)
    idc = jnp.minimum(idx, n_valid - 1)
    item_e = jnp.minimum(jnp.sum((idc[:, None] >= it_end[None, :]).astype(I32), axis=1), N_EXPERTS - 1).astype(I32)
    chunk = idc - it_start[item_e]
    item_start = ((sb_start[item_e] + per_item * chunk) * MOE_SUB).astype(I32)
    item_ns = (jnp.clip(n_sb[item_e] - per_item * chunk, 0, per_item) * item_valid).astype(I32)
    return item_e, item_ns, item_valid, item_start, slot_pair


def _prep_params(ln_in_g, ln_in_b, w_in, b_att_f, g_att_q, g_att_k, b_m_i, b_m_f, g_m_out, w_out,
                 ln1_g, ln1_b, w_router, b_router, ln2_g, ln2_b):
    d = w_in.shape[0]
    d_att = N_ATT_HEADS * ATT_HEAD_DIM
    d_mqk = N_MLSTM_HEADS * MLSTM_QK_DIM
    d_mv = N_MLSTM_HEADS * MLSTM_V_DIM
    sizes = (d_att, d_att, d_att, N_ATT_HEADS, d_mqk, d_mqk, d_mv, N_MLSTM_HEADS, N_MLSTM_HEADS, d_mv)
    offs = [0]
    for s in sizes:
        offs.append(offs[-1] + s)
    cols = lambda i: w_in[:, offs[i]:offs[i + 1]]
    w_main = jnp.concatenate([cols(0), cols(1), cols(2), cols(4), cols(5), cols(6), cols(9)], axis=1).astype(BF16)
    w_gate = jnp.concatenate([cols(3), cols(7), cols(8), jnp.zeros((d, LANES - N_GATES), F32)], axis=1)
    wg_hi, wg_lo = _split2(w_gate)
    gate_bias = jnp.concatenate([b_att_f, b_m_i, b_m_f, jnp.zeros((LANES - N_GATES,), F32)]).reshape(1, LANES)
    w_r = jnp.concatenate([w_router, jnp.zeros((d, LANES - N_EXPERTS), F32)], axis=1)
    wr_hi, wr_lo = _split2(w_r)
    b_r = jnp.concatenate([b_router, jnp.full((LANES - N_EXPERTS,), NEG, F32)]).reshape(1, LANES)
    return dict(
        ln_in_g=ln_in_g.reshape(1, d), ln_in_b=ln_in_b.reshape(1, d), w_main=w_main, wg_hi=wg_hi, wg_lo=wg_lo,
        gate_bias=gate_bias, g_att_q=g_att_q.reshape(1, -1), g_att_k=g_att_k.reshape(1, -1),
        g_m_out=g_m_out.reshape(1, -1), w_out=w_out.astype(BF16), ln1_g=ln1_g.reshape(1, d),
        ln1_b=ln1_b.reshape(1, d), wr_hi=wr_hi, wr_lo=wr_lo, b_router=b_r,
        ln2_g=ln2_g.reshape(1, d), ln2_b=ln2_b.reshape(1, d))


def kernel(x_prompt, x_sample, cache_k, cache_v, cache_logf, state_C, state_n, state_m, page_table,
           meta_tokens, ln_in_g, ln_in_b, w_in, b_att_f, g_att_q, g_att_k, b_m_i, b_m_f, g_m_out, w_out,
           ln1_g, ln1_b, w_router, b_router, w_gate_up, b_gate_up, w_down, b_down, ln2_g, ln2_b):
    depth = w_in.shape[0]
    assert depth == 1, "single-layer trunk only"
    nb, seq, d = x_prompt.shape
    db = x_sample.shape[0]
    assert x_sample.shape[1] == 1, "one new token per sample"
    alpha = (2.0 * depth) ** 0.25
    d_att = N_ATT_HEADS * ATT_HEAD_DIM
    hm = N_MLSTM_HEADS
    prm = _prep_params(ln_in_g, ln_in_b, w_in[0], b_att_f[0], g_att_q[0], g_att_k[0], b_m_i[0], b_m_f[0],
                       g_m_out[0], w_out[0], ln1_g[0], ln1_b[0], w_router[0], b_router[0], ln2_g[0], ln2_b[0])

    length = N_META + seq
    lp = pl.cdiv(length, CHUNK) * CHUNK
    meta = jnp.broadcast_to(meta_tokens[None], (nb, N_META, d))
    xp = jnp.pad(x_prompt, ((0, 0), (N_META, lp - length), (0, 0)))
    xp = lax.dynamic_update_slice(xp, meta, (0, 0, 0))
    (hp, qa, k32, kb, v32, vb, gates, qm, km, vm, og) = _proj(xp, length, lp // 8, prm)
    cum_l, cum_lt, gates_t, k_aug, q_aug = _gate_scan(gates)
    att_p = _fox_prompt(qa, q_aug, kb, k_aug, vb)
    hm_p, c_p, n_p, m_p = _mlstm_prompt(qm, km, vm, gates, cum_l, gates_t, cum_lt)
    rows_p = nb * lp
    h1_p, h1p_p, e_p, gate_p = _mix(att_p.reshape(rows_p, d_att), hm_p.reshape(rows_p, -1), og.reshape(rows_p, -1),
                                    hp.reshape(rows_p, d), prm, 256, alpha)

    xs = x_sample.reshape(1, db, d)
    (hs, qa_s, k32_s, _, v32_s, _, gates_s, qm_s, km_s, vm_s, og_s) = _proj(xs, db, db, prm)
    n_pool, page = cache_k.shape[1], cache_k.shape[2]
    nh, dh = N_ATT_HEADS, ATT_HEAD_DIM
    lf_s = gates_s[0, :, GATE_LF_ATT:GATE_LF_ATT + nh]
    att_s = _paged_attention(
        qa_s.astype(F32).reshape(db, nh, dh), k32_s.reshape(db, nh, dh), v32_s.reshape(db, nh, dh),
        jnp.tile(lf_s, (1, page)).reshape(db, 1, page * nh), cache_k, cache_v,
        cache_logf[0].reshape(n_pool, 1, page * nh), page_table)
    ig_s = gates_s[0, :, GATE_IG:GATE_IG + hm].reshape(db, hm, 1, 1)
    lfm_s = gates_s[0, :, GATE_LF_M:GATE_LF_M + hm].reshape(db, hm, 1, 1)
    hm_s, c_s, n_s, m_s = _mlstm_step(
        qm_s.astype(F32).reshape(db, hm, MLSTM_QK_DIM, 1), km_s.astype(F32).reshape(db, hm, MLSTM_QK_DIM, 1),
        vm_s.astype(F32).reshape(db, hm, 1, MLSTM_V_DIM), ig_s, lfm_s, state_C[0],
        state_n[0].reshape(db, hm, MLSTM_QK_DIM, 1), state_m[0].reshape(db, hm, 1, 1))
    h1_s, h1p_s, e_s, gate_s = _mix(att_s.reshape(db, d_att), hm_s.reshape(db, -1), og_s.reshape(db, -1),
                                    hs.reshape(db, d), prm, db, alpha)

    n_tok = rows_p + db
    x_packed = jnp.concatenate([h1p_p, h1p_s], axis=0)
    top_e = jnp.concatenate([e_p, e_s], axis=0)[:, :TOP_K]
    item_e, item_ns, item_valid, item_start, slot_pair = _route(top_e)
    y_pairs = _moe(x_packed, item_e, item_ns, item_valid, item_start, slot_pair,
                   w_gate_up[0], b_gate_up[0], w_down[0], b_down[0])
    y_prompt = _combine(y_pairs, gate_p, h1_p, prm["ln2_g"], prm["ln2_b"], alpha, n_tok=n_tok, tok0=0,
                        group_rows=lp, skip=N_META, out_rows=seq, tm=256)
    y_sample = _combine(y_pairs, gate_s, h1_s, prm["ln2_g"], prm["ln2_b"], alpha, n_tok=n_tok, tok0=rows_p,
                        group_rows=db, skip=0, out_rows=db, tm=db).reshape(db, 1, d)
    k_prompt = k32[None]
    v_prompt = v32[None]
    lf_prompt = gates[:, :length, GATE_LF_ATT:GATE_LF_ATT + N_ATT_HEADS][None]
    return (y_prompt, y_sample, k_prompt, v_prompt, lf_prompt,
            c_p[None], n_p.reshape(1, nb, hm, MLSTM_QK_DIM), m_p[:, :, 0, 0][None],
            k32_s.reshape(1, db, 1, N_ATT_HEADS, ATT_HEAD_DIM), v32_s.reshape(1, db, 1, N_ATT_HEADS, ATT_HEAD_DIM),
            lf_s.reshape(1, db, 1, nh),
            c_s[None], n_s.reshape(1, db, hm, MLSTM_QK_DIM), m_s.reshape(1, db, hm))
```

```python
import functools

import jax
import jax.numpy as jnp
from jax import lax
from jax.experimental import pallas as pl
from jax.experimental.pallas import tpu as pltpu

F32 = jnp.float32
BF16 = jnp.bfloat16
I32 = jnp.int32

N_META = 16
CHUNK = 128
ATT_HEAD_DIM = 128
N_ATT_HEADS = 8
N_MLSTM_HEADS = 4
MLSTM_QK_DIM = 128
MLSTM_V_DIM = 256
N_EXPERTS = 32
TOP_K = 4
LOG2_TOP_K = 2
SWIGLU_ALPHA = 1.702
SWIGLU_LIMIT = 7.0
LN_EPS = 1e-5
RMS_EPS = 1e-6
NEG = -1e30

LANES = 128
GATE_LF_ATT = 0
GATE_IG = N_ATT_HEADS
GATE_LF_M = GATE_IG + N_MLSTM_HEADS
N_GATES = GATE_LF_M + N_MLSTM_HEADS

VMEM_LIMIT = 56 * 1024 * 1024

MOE_ROWS = 1152
MOE_SUB = 128
MOE_FT = 256
MOE_BLOCK = 512
PAGES_PER_STEP = 8
FOX_TQ = 256


def _cparams(*sem):
    return pltpu.CompilerParams(dimension_semantics=sem, vmem_limit_bytes=VMEM_LIMIT)


def _split2(x):
    hi = x.astype(BF16)
    lo = (x - hi.astype(F32)).astype(BF16)
    return hi, lo


def _split3(x):
    hi = x.astype(BF16)
    r1 = x - hi.astype(F32)
    mid = r1.astype(BF16)
    lo = (r1 - mid.astype(F32)).astype(BF16)
    return hi, mid, lo


def _dot(a, b):
    return jnp.dot(a, b, preferred_element_type=F32)


def _dot_nt(a, b):
    return lax.dot_general(a, b, (((1,), (1,)), ((), ())), preferred_element_type=F32)


def _dot_tn(a, b):
    return lax.dot_general(a, b, (((0,), (0,)), ((), ())), preferred_element_type=F32)


def _dot3(x_f32, w_hi, w_lo):
    x_hi, x_lo = _split2(x_f32)
    return _dot(x_hi, w_hi) + (_dot(x_lo, w_hi) + _dot(x_hi, w_lo))


def _layer_norm(x, g, b):
    mu = jnp.mean(x, axis=-1, keepdims=True)
    xc = x - mu
    var = jnp.mean(xc * xc, axis=-1, keepdims=True)
    return xc * lax.rsqrt(var + LN_EPS) * g + b


def _log_sigmoid(z):
    return jnp.minimum(z, 0.0) - jnp.log(1.0 + jnp.exp(-jnp.abs(z)))


def _sigmoid(z):
    return 1.0 / (1.0 + jnp.exp(-z))


def _proj_kernel(x_ref, lng_ref, lnb_ref, w_ref, wgh_ref, wgl_ref, gb_ref, gq_ref, gk_ref,
                 h_ref, q_ref, k32_ref, kb_ref, v32_ref, vb_ref, g_ref, qm_ref, km_ref, vm_ref, og_ref,
                 *, tm, valid_len):
    d_att = N_ATT_HEADS * ATT_HEAD_DIM
    d_mqk = N_MLSTM_HEADS * MLSTM_QK_DIM
    d_mv = N_MLSTM_HEADS * MLSTM_V_DIM
    h = _layer_norm(x_ref[0], lng_ref[...], lnb_ref[...])
    h_ref[0] = h
    hb = h.astype(BF16)

    def seg(c0, width):
        return _dot(hb, w_ref[:, c0:c0 + width])

    scale = ATT_HEAD_DIM ** -0.5
    qa = seg(0, d_att)
    ka = seg(d_att, d_att)
    for hh in range(N_ATT_HEADS):
        sl = slice(hh * ATT_HEAD_DIM, (hh + 1) * ATT_HEAD_DIM)
        qh = qa[:, sl]
        qn = qh * lax.rsqrt(jnp.mean(qh * qh, axis=-1, keepdims=True) + RMS_EPS) * gq_ref[...]
        q_ref[0, :, sl] = (qn * scale).astype(BF16)
        kh = ka[:, sl]
        kn = kh * lax.rsqrt(jnp.mean(kh * kh, axis=-1, keepdims=True) + RMS_EPS) * gk_ref[...]
        k32_ref[0, :, sl] = kn
        kb_ref[0, :, sl] = kn.astype(BF16)
    va = seg(2 * d_att, d_att)
    v32_ref[0] = va
    vb_ref[0] = va.astype(BF16)
    c0 = 3 * d_att
    qm_ref[0] = seg(c0, d_mqk).astype(BF16)
    km_ref[0] = (seg(c0 + d_mqk, d_mqk) * (MLSTM_QK_DIM ** -0.5)).astype(BF16)
    vm_ref[0] = seg(c0 + 2 * d_mqk, d_mv).astype(BF16)
    og_ref[0] = _sigmoid(seg(c0 + 2 * d_mqk + d_mv, d_mv))

    z = _dot3(h, wgh_ref[...], wgl_ref[...]) + gb_ref[...]
    lane = lax.broadcasted_iota(I32, z.shape, 1)
    row = pl.program_id(1) * tm + lax.broadcasted_iota(I32, z.shape, 0)
    is_ig = jnp.logical_and(lane >= GATE_IG, lane < GATE_LF_M)
    gates = jnp.where(is_ig, z, jnp.where(lane < N_GATES, _log_sigmoid(z), 0.0))
    gates = jnp.where(row < valid_len, gates, jnp.where(is_ig, NEG, 0.0))
    g_ref[0] = gates


def _proj(x, valid_len, tm, prm):
    nb, lp, d = x.shape
    d_att = N_ATT_HEADS * ATT_HEAD_DIM
    d_mqk = N_MLSTM_HEADS * MLSTM_QK_DIM
    d_mv = N_MLSTM_HEADS * MLSTM_V_DIM
    n_main = prm["w_main"].shape[1]
    grid = (nb, lp // tm)
    row = lambda w: pl.BlockSpec((1, tm, w), lambda b, i: (b, i, 0))
    const = lambda r, c: pl.BlockSpec((r, c), lambda b, i: (0, 0), pipeline_mode=pl.Buffered(1))
    out_shape = (
        jax.ShapeDtypeStruct((nb, lp, d), F32),
        jax.ShapeDtypeStruct((nb, lp, d_att), BF16),
        jax.ShapeDtypeStruct((nb, valid_len, d_att), F32),
        jax.ShapeDtypeStruct((nb, lp, d_att), BF16),
        jax.ShapeDtypeStruct((nb, valid_len, d_att), F32),
        jax.ShapeDtypeStruct((nb, lp, d_att), BF16),
        jax.ShapeDtypeStruct((nb, lp, LANES), F32),
        jax.ShapeDtypeStruct((nb, lp, d_mqk), BF16),
        jax.ShapeDtypeStruct((nb, lp, d_mqk), BF16),
        jax.ShapeDtypeStruct((nb, lp, d_mv), BF16),
        jax.ShapeDtypeStruct((nb, lp, d_mv), F32),
    )
    out_specs = (row(d), row(d_att), row(d_att), row(d_att), row(d_att), row(d_att), row(LANES),
                 row(d_mqk), row(d_mqk), row(d_mv), row(d_mv))
    return pl.pallas_call(
        functools.partial(_proj_kernel, tm=tm, valid_len=valid_len),
        grid=grid,
        in_specs=[row(d), const(1, d), const(1, d), const(d, n_main), const(d, LANES), const(d, LANES),
                  const(1, LANES), const(1, ATT_HEAD_DIM), const(1, ATT_HEAD_DIM)],
        out_specs=out_specs,
        out_shape=out_shape,
        compiler_params=_cparams("parallel", "arbitrary"),
        name="proj",
    )(x, prm["ln_in_g"], prm["ln_in_b"], prm["w_main"], prm["wg_hi"], prm["wg_lo"], prm["gate_bias"],
      prm["g_att_q"], prm["g_att_k"])


def _gate_scan_kernel(g_ref, cl_ref, clt_ref, gt_ref, ka_ref, qa_ref, *, n_chunks):
    r = lax.broadcasted_iota(I32, (CHUNK, CHUNK), 0)
    c = lax.broadcasted_iota(I32, (CHUNK, CHUNK), 1)
    tri = jnp.where(c <= r, 1.0, 0.0).astype(BF16)
    carry = jnp.zeros((1, LANES), F32)
    for ci in range(n_chunks):
        sl = slice(ci * CHUNK, (ci + 1) * CHUNK)
        g = g_ref[0, sl, :]
        hi, mid, lo = _split3(g)
        loc = _dot(tri, hi) + (_dot(tri, mid) + _dot(tri, lo))
        glob = loc + carry
        carry = glob[CHUNK - 1:CHUNK, :]
        cl_ref[0, sl, :] = loc
        clt_ref[0, ci] = loc.T
        gt_ref[0, ci] = g.T
        for hh in range(N_ATT_HEADS):
            cum = glob[:, GATE_LF_ATT + hh:GATE_LF_ATT + hh + 1]
            t0 = cum.astype(BF16).astype(F32)
            r1 = cum - t0
            t1 = r1.astype(BF16).astype(F32)
            t2 = (r1 - t1).astype(BF16).astype(F32)
            hs = slice(hh * ATT_HEAD_DIM, (hh + 1) * ATT_HEAD_DIM)
            ones = jnp.where(c < 3, 1.0, 0.0)
            terms = jnp.where(c == 3, t0, jnp.where(c == 4, t1, jnp.where(c == 5, t2, 0.0)))
            qa_ref[0, sl, hs] = (ones + terms).astype(BF16)
            ones_k = jnp.where(jnp.logical_and(c >= 3, c < 6), 1.0, 0.0)
            terms_k = jnp.where(c == 0, t0, jnp.where(c == 1, t1, jnp.where(c == 2, t2, 0.0)))
            ka_ref[0, sl, hs] = (ones_k - terms_k).astype(BF16)


def _gate_scan(gates):
    nb, lp, _ = gates.shape
    nc = lp // CHUNK
    d_att = N_ATT_HEADS * ATT_HEAD_DIM
    col = pl.BlockSpec((1, lp, LANES), lambda b: (b, 0, 0))
    rowt = pl.BlockSpec((1, nc, LANES, CHUNK), lambda b: (b, 0, 0, 0))
    aug = pl.BlockSpec((1, lp, d_att), lambda b: (b, 0, 0))
    return pl.pallas_call(
        functools.partial(_gate_scan_kernel, n_chunks=nc),
        grid=(nb,),
        in_specs=[col],
        out_specs=(col, rowt, rowt, aug, aug),
        out_shape=(jax.ShapeDtypeStruct((nb, lp, LANES), F32),
                   jax.ShapeDtypeStruct((nb, nc, LANES, CHUNK), F32),
                   jax.ShapeDtypeStruct((nb, nc, LANES, CHUNK), F32),
                   jax.ShapeDtypeStruct((nb, lp, d_att), BF16),
                   jax.ShapeDtypeStruct((nb, lp, d_att), BF16)),
        compiler_params=_cparams("parallel"),
        name="gate_scan",
    )(gates)


def _fox_kernel(q_ref, qa_ref, k_ref, ka_ref, v_ref, o_ref, kx_sc, vt_sc, qx_sc, m_sc, l_sc, acc_sc,
                *, n_chunks, tq):
    ti = pl.program_id(1)
    dh = ATT_HEAD_DIM

    @pl.when(ti == 0)
    def _():
        for j in range(n_chunks):
            rows = slice(j * CHUNK, (j + 1) * CHUNK)
            for hh in range(N_ATT_HEADS):
                sl = slice(hh * dh, (hh + 1) * dh)
                kx_sc[j, :, 2 * hh * dh:(2 * hh + 1) * dh] = k_ref[0, rows, sl]
                kx_sc[j, :, (2 * hh + 1) * dh:(2 * hh + 2) * dh] = ka_ref[0, rows, sl]
                vt_sc[hh, j] = v_ref[0, rows, sl].astype(F32).T.astype(BF16)

    for hh in range(N_ATT_HEADS):
        sl = slice(hh * dh, (hh + 1) * dh)
        qx_sc[:, 2 * hh * dh:(2 * hh + 1) * dh] = q_ref[0, :, sl]
        qx_sc[:, (2 * hh + 1) * dh:(2 * hh + 2) * dh] = qa_ref[0, :, sl]
    m_sc[...] = jnp.full(m_sc.shape, NEG, F32)
    l_sc[...] = jnp.zeros(l_sc.shape, F32)
    acc_sc[...] = jnp.zeros(acc_sc.shape, F32)
    key = lax.broadcasted_iota(I32, (CHUNK, tq), 0)
    qry = lax.broadcasted_iota(I32, (CHUNK, tq), 1)

    def step(j, masked):
        scores = []
        for hh in range(N_ATT_HEADS):
            xs = slice(2 * hh * dh, (2 * hh + 2) * dh)
            scores.append(_dot_nt(kx_sc[j, :, xs], qx_sc[:, xs]))
        probs, alphas = [], []
        for hh in range(N_ATT_HEADS):
            s = scores[hh]
            if masked:
                s = jnp.where(j * CHUNK + key <= ti * tq + qry, s, NEG)
            m_old = m_sc[hh]
            m_new = jnp.maximum(m_old, jnp.max(s, axis=0, keepdims=True))
            p = jnp.exp(s - m_new)
            alpha = jnp.exp(m_old - m_new)
            l_sc[hh] = alpha * l_sc[hh] + jnp.sum(p, axis=0, keepdims=True)
            m_sc[hh] = m_new
            probs.append(p.astype(BF16))
            alphas.append(alpha)
        for hh in range(N_ATT_HEADS):
            acc_sc[hh] = alphas[hh] * acc_sc[hh] + _dot(vt_sc[hh, j], probs[hh])

    def body(j, carry):
        step(j, False)
        return carry

    per = tq // CHUNK
    lax.fori_loop(0, ti * per, body, 0)
    for dj in range(per):
        j = ti * per + dj

        @pl.when(j < n_chunks)
        def _():
            step(j, True)

    for hh in range(N_ATT_HEADS):
        sl = slice(hh * dh, (hh + 1) * dh)
        o_ref[0, :, sl] = (acc_sc[hh] / l_sc[hh]).T.astype(o_ref.dtype)


def _fox_prompt(q, q_aug, kb, k_aug, vb):
    nb, lp, d_att = q.shape
    nc = lp // CHUNK
    tq = FOX_TQ
    blk = pl.BlockSpec((1, tq, d_att), lambda b, i: (b, i, 0))
    full = pl.BlockSpec((1, lp, d_att), lambda b, i: (b, 0, 0))
    nh, dh = N_ATT_HEADS, ATT_HEAD_DIM
    return pl.pallas_call(
        functools.partial(_fox_kernel, n_chunks=nc, tq=tq),
        grid=(nb, pl.cdiv(lp, tq)),
        in_specs=[blk, blk, full, full, full],
        out_specs=blk,
        out_shape=jax.ShapeDtypeStruct((nb, lp, d_att), BF16),
        scratch_shapes=[pltpu.VMEM((nc, CHUNK, 2 * d_att), BF16),
                        pltpu.VMEM((nh, nc, dh, CHUNK), BF16),
                        pltpu.VMEM((tq, 2 * d_att), BF16),
                        pltpu.VMEM((nh, 1, tq), F32), pltpu.VMEM((nh, 1, tq), F32),
                        pltpu.VMEM((nh, dh, tq), F32)],
        compiler_params=_cparams("parallel", "arbitrary"),
        name="fox_prompt",
    )(q, q_aug, kb, k_aug, vb)


def _mlstm_kernel(q_ref, k_ref, v_ref, g_ref, cl_ref, gt_ref, clt_ref,
                  h_ref, c_out_ref, n_out_ref, m_out_ref, c_sc, n_sc, m_sc):
    ci = pl.program_id(1)

    @pl.when(ci == 0)
    def _():
        c_sc[...] = jnp.zeros(c_sc.shape, F32)
        n_sc[...] = jnp.zeros(n_sc.shape, F32)
        m_sc[...] = jnp.zeros(m_sc.shape, F32)

    r = lax.broadcasted_iota(I32, (CHUNK, CHUNK), 0)
    c = lax.broadcasted_iota(I32, (CHUNK, CHUNK), 1)
    causal = c <= r
    last = slice(CHUNK - 1, CHUNK)
    heads = range(N_MLSTM_HEADS)
    qk = [slice(hh * MLSTM_QK_DIM, (hh + 1) * MLSTM_QK_DIM) for hh in heads]
    vv = [slice(hh * MLSTM_V_DIM, (hh + 1) * MLSTM_V_DIM) for hh in heads]
    qh = [q_ref[0, :, qk[hh]] for hh in heads]
    kh = [k_ref[0, :, qk[hh]] for hh in heads]
    vh = [v_ref[0, :, vv[hh]] for hh in heads]
    c0 = [c_sc[hh] for hh in heads]
    qk_raw = [_dot_nt(qh[hh], kh[hh]) for hh in heads]
    q_c0 = [_dot(qh[hh], c0[hh].astype(BF16)) for hh in heads]

    s_bf, kw_bf, stash = [], [], []
    for hh in heads:
        ig_l, lf_l = GATE_IG + hh, GATE_LF_M + hh
        b_col = cl_ref[0, :, lf_l:lf_l + 1]
        ig_col = g_ref[0, :, ig_l:ig_l + 1]
        b_row = clt_ref[0, 0, lf_l:lf_l + 1, :]
        ig_row = gt_ref[0, 0, ig_l:ig_l + 1, :]
        m0 = m_sc[hh][:, 0:1]
        n0 = n_sc[hh]
        dmat = jnp.where(causal, b_col - b_row + ig_row, NEG)
        inter = b_col + m0
        mt = jnp.maximum(inter, jnp.max(dmat, axis=-1, keepdims=True))
        w_inter = jnp.exp(inter - mt)
        s = qk_raw[hh] * jnp.exp(dmat - mt)
        den = jnp.sum(s, axis=-1, keepdims=True) + w_inter * jnp.sum(qh[hh].astype(F32) * n0, axis=-1, keepdims=True)
        m_new = mt[last, :]
        b_last = b_col[last, :]
        w_state = jnp.exp(b_last - b_col + ig_col - m_new)
        decay = jnp.exp(b_last + m0 - m_new)
        kw = kh[hh].astype(F32) * w_state
        s_bf.append(s.astype(BF16))
        kw_bf.append(kw.astype(BF16))
        stash.append((w_inter, den, mt, m_new, decay, decay * n0 + jnp.sum(kw, axis=0, keepdims=True)))

    s_v = [_dot(s_bf[hh], vh[hh]) for hh in heads]
    kw_v = [_dot_tn(kw_bf[hh], vh[hh]) for hh in heads]
    for hh in heads:
        w_inter, den, mt, m_new, decay, n_new = stash[hh]
        num = s_v[hh] + w_inter * q_c0[hh]
        h_ref[0, :, vv[hh]] = num / jnp.maximum(jnp.abs(den), jnp.exp(-mt))
        c_new = decay * c0[hh] + kw_v[hh]
        c_sc[hh] = c_new
        n_sc[hh] = n_new
        m_sc[hh] = jnp.broadcast_to(m_new, (1, LANES))
        c_out_ref[0, hh] = c_new
        n_out_ref[0, hh] = n_new
        m_out_ref[0, hh] = jnp.broadcast_to(m_new, (1, LANES))


def _mlstm_prompt(qm, km, vm, gates, cum_l, gates_t, cum_lt):
    nb, lp, d_mqk = qm.shape
    d_mv = vm.shape[-1]
    nc = lp // CHUNK
    hm = N_MLSTM_HEADS
    blk = lambda w: pl.BlockSpec((1, CHUNK, w), lambda b, i: (b, i, 0))
    tblk = pl.BlockSpec((1, 1, LANES, CHUNK), lambda b, i: (b, i, 0, 0))
    return pl.pallas_call(
        _mlstm_kernel,
        grid=(nb, nc),
        in_specs=[blk(d_mqk), blk(d_mqk), blk(d_mv), blk(LANES), blk(LANES), tblk, tblk],
        out_specs=(blk(d_mv),
                   pl.BlockSpec((1, hm, MLSTM_QK_DIM, MLSTM_V_DIM), lambda b, i: (b, 0, 0, 0)),
                   pl.BlockSpec((1, hm, 1, MLSTM_QK_DIM), lambda b, i: (b, 0, 0, 0)),
                   pl.BlockSpec((1, hm, 1, LANES), lambda b, i: (b, 0, 0, 0))),
        out_shape=(jax.ShapeDtypeStruct((nb, lp, d_mv), F32),
                   jax.ShapeDtypeStruct((nb, hm, MLSTM_QK_DIM, MLSTM_V_DIM), F32),
                   jax.ShapeDtypeStruct((nb, hm, 1, MLSTM_QK_DIM), F32),
                   jax.ShapeDtypeStruct((nb, hm, 1, LANES), F32)),
        scratch_shapes=[pltpu.VMEM((hm, MLSTM_QK_DIM, MLSTM_V_DIM), F32),
                        pltpu.VMEM((hm, 1, MLSTM_QK_DIM), F32),
                        pltpu.VMEM((hm, 1, LANES), F32)],
        compiler_params=_cparams("parallel", "arbitrary"),
        name="mlstm_prompt",
    )(qm, km, vm, gates, cum_l, gates_t, cum_lt)


def _paged_kernel(pt_ref, q_ref, kn_ref, vn_ref, lfn_ref, *refs, pps):
    k_refs = refs[:pps]
    v_refs = refs[pps:2 * pps]
    lf_refs = refs[2 * pps:3 * pps]
    o_ref, lf_sc, bias_sc, m_sc, l_sc, acc_sc, carry_sc = refs[3 * pps:]
    g = pl.program_id(1)
    nh, dh = N_ATT_HEADS, ATT_HEAD_DIM
    page = k_refs[0].shape[0]
    width = page * nh
    q8 = q_ref[0]

    @pl.when(g == 0)
    def _():
        m_sc[...] = jnp.sum(q8 * kn_ref[0], axis=-1, keepdims=True)
        l_sc[...] = jnp.ones(l_sc.shape, F32)
        acc_sc[...] = vn_ref[0]
        carry_sc[...] = jnp.zeros(carry_sc.shape, F32)

    for u in range(pps):
        lf_sc[u:u + 1, :] = lf_refs[u][...]
    lf = lf_sc[...]
    lane = lax.broadcasted_iota(I32, (pps, width), 1)
    incl = lf
    k = nh
    while k < width:
        incl = incl + jnp.where(lane < width - k, pltpu.roll(incl, width - k, axis=1), 0.0)
        k *= 2
    tot = jnp.where(lane < nh, incl, 0.0)
    k = nh
    while k < width:
        tot = tot + pltpu.roll(tot, k, axis=1)
        k *= 2
    run = carry_sc[...]
    for u in range(pps):
        bias_sc[u:u + 1, :] = run
        run = run + tot[u:u + 1, :]
    carry_sc[...] = run
    bias = bias_sc[...] + (incl - lf) + lfn_ref[0]

    own = lax.broadcasted_iota(I32, (nh, width), 1) % nh == lax.broadcasted_iota(I32, (nh, width), 0)
    q_bf = q8.astype(BF16)
    scores = []
    m_old = m_sc[...]
    m_new = m_old
    for u in range(pps):
        kf = k_refs[u][...].reshape(width, dh).astype(BF16)
        s = jnp.where(own, _dot_nt(q_bf, kf) + bias[u:u + 1, :], NEG)
        m_new = jnp.maximum(m_new, jnp.max(s, axis=-1, keepdims=True))
        scores.append(s)
    alpha = jnp.exp(m_old - m_new)
    l_new = alpha * l_sc[...]
    acc = alpha * acc_sc[...]
    for u in range(pps):
        p = jnp.exp(scores[u] - m_new)
        l_new = l_new + jnp.sum(p, axis=-1, keepdims=True)
        acc = acc + _dot(p.astype(BF16), v_refs[u][...].reshape(width, dh).astype(BF16))
    m_sc[...] = m_new
    l_sc[...] = l_new
    acc_sc[...] = acc

    @pl.when(g == pl.num_programs(1) - 1)
    def _():
        o_ref[0] = acc / l_new


def _paged_attention(q, k_new, v_new, lf_new_flat, cache_k, cache_v, cache_lf_flat, page_table):
    db, n_pages = page_table.shape
    _, n_pool, page, nh, dh = cache_k.shape
    width = page * nh
    pps = PAGES_PER_STEP
    steps = n_pages // pps
    tok = lambda: pl.BlockSpec((1, nh, dh), lambda b, g, pt: (b, 0, 0))

    def page_id(b, g, pt, u):
        return pt[b * n_pages + (n_pages - 1 - (g * pps + u))]

    kv_specs = [pl.BlockSpec((None, None, page, nh, dh),
                             functools.partial(lambda b, g, pt, u: (0, page_id(b, g, pt, u), 0, 0, 0), u=u))
                for u in range(pps)]
    lf_specs = [pl.BlockSpec((None, 1, width),
                             functools.partial(lambda b, g, pt, u: (page_id(b, g, pt, u), 0, 0), u=u))
                for u in range(pps)]
    grid_spec = pltpu.PrefetchScalarGridSpec(
        num_scalar_prefetch=1,
        grid=(db, steps),
        in_specs=[tok(), tok(), tok(), pl.BlockSpec((1, 1, width), lambda b, g, pt: (b, 0, 0))]
        + kv_specs + kv_specs + lf_specs,
        out_specs=tok(),
        scratch_shapes=[pltpu.VMEM((pps, width), F32), pltpu.VMEM((pps, width), F32),
                        pltpu.VMEM((nh, 1), F32), pltpu.VMEM((nh, 1), F32),
                        pltpu.VMEM((nh, dh), F32), pltpu.VMEM((1, width), F32)],
    )
    return pl.pallas_call(
        functools.partial(_paged_kernel, pps=pps),
        grid_spec=grid_spec,
        out_shape=jax.ShapeDtypeStruct((db, nh, dh), F32),
        compiler_params=_cparams("parallel", "arbitrary"),
        name="paged_fox",
    )(page_table.reshape(-1), q, k_new, v_new, lf_new_flat, *([cache_k] * pps), *([cache_v] * pps),
      *([cache_lf_flat] * pps))


def _mstep_kernel(q_ref, k_ref, v_ref, ig_ref, lf_ref, c_ref, n_ref, m_ref,
                  h_ref, c_out_ref, n_out_ref, m_out_ref):
    for bb in range(q_ref.shape[0]):
        for hh in range(N_MLSTM_HEADS):
            q = q_ref[bb, hh].astype(F32)
            k = k_ref[bb, hh].astype(F32)
            v = v_ref[bb, hh].astype(F32)
            ig = ig_ref[bb, hh]
            lf = lf_ref[bb, hh]
            c0 = c_ref[bb, hh]
            n0 = n_ref[bb, hh]
            m0 = m_ref[bb, hh]
            inter = lf + m0
            m = jnp.maximum(inter, ig)
            w_intra = jnp.exp(ig - m)
            w_inter = jnp.exp(inter - m)
            s = jnp.sum(q * k, axis=0, keepdims=True) * w_intra
            num = s * v + w_inter * jnp.sum(q * c0, axis=0, keepdims=True)
            den = s + w_inter * jnp.sum(q * n0, axis=0, keepdims=True)
            h_ref[bb, hh] = num / jnp.maximum(jnp.abs(den), jnp.exp(-m))
            c_out_ref[bb, hh] = w_inter * c0 + w_intra * (k * v)
            n_out_ref[bb, hh] = w_inter * n0 + w_intra * k
            m_out_ref[bb, hh] = m


MSTEP_ROWS = 4


def _mlstm_step(q_col, k_col, v_row, ig, lf, state_c, state_n_col, state_m):
    db, hm, dk, dv = state_c.shape
    tb = MSTEP_ROWS if db % MSTEP_ROWS == 0 else 1
    spec = lambda a, b_: pl.BlockSpec((tb, hm, a, b_), lambda i: (i, 0, 0, 0))
    return pl.pallas_call(
        _mstep_kernel,
        grid=(db // tb,),
        in_specs=[spec(dk, 1), spec(dk, 1), spec(1, dv), spec(1, 1), spec(1, 1), spec(dk, dv), spec(dk, 1), spec(1, 1)],
        out_specs=(spec(1, dv), spec(dk, dv), spec(dk, 1), spec(1, 1)),
        out_shape=(jax.ShapeDtypeStruct((db, hm, 1, dv), F32), jax.ShapeDtypeStruct((db, hm, dk, dv), F32),
                   jax.ShapeDtypeStruct((db, hm, dk, 1), F32), jax.ShapeDtypeStruct((db, hm, 1, 1), F32)),
        compiler_params=_cparams("parallel"),
        name="mlstm_step",
    )(q_col, k_col, v_row, ig, lf, state_c, state_n_col, state_m)


def _mix_kernel(att_ref, hm_ref, og_ref, h_ref, gm_ref, w_ref, lg_ref, lb_ref, wrh_ref, wrl_ref, br_ref,
                h1_ref, h1p_ref, e_ref, gate_ref, *, alpha):
    d_att = att_ref.shape[-1]
    parts = []
    for hh in range(N_MLSTM_HEADS):
        vv = slice(hh * MLSTM_V_DIM, (hh + 1) * MLSTM_V_DIM)
        x = hm_ref[:, vv]
        xn = x * lax.rsqrt(jnp.mean(x * x, axis=-1, keepdims=True) + RMS_EPS) * gm_ref[:, vv]
        parts.append((og_ref[:, vv] * xn).astype(BF16))
    mix = _dot(att_ref[...].astype(BF16), w_ref[0:d_att, :])
    for hh in range(N_MLSTM_HEADS):
        r0 = d_att + hh * MLSTM_V_DIM
        mix = mix + _dot(parts[hh], w_ref[r0:r0 + MLSTM_V_DIM, :])
    h1 = _layer_norm(alpha * h_ref[...] + mix, lg_ref[...], lb_ref[...])
    h1_ref[...] = h1
    half = h1.shape[-1] // 2
    hi = lax.bitcast_convert_type(h1[:, :half].astype(BF16).astype(F32), jnp.uint32)
    lo = lax.bitcast_convert_type(h1[:, half:].astype(BF16).astype(F32), jnp.uint32)
    h1p_ref[...] = hi | (lo >> 16)

    logits = _dot3(h1, wrh_ref[...], wrl_ref[...]) + br_ref[...]
    lane = lax.broadcasted_iota(I32, logits.shape, 1).astype(F32)
    e_out = jnp.zeros(logits.shape, F32)
    top = []
    for kk in range(TOP_K):
        mx = jnp.max(logits, axis=-1, keepdims=True)
        idx = jnp.min(jnp.where(logits == mx, lane, float(LANES)), axis=-1, keepdims=True)
        logits = jnp.where(lane == idx, -jnp.inf, logits)
        e_out = jnp.where(lane == float(kk), idx, e_out)
        top.append(mx)
    ex = [jnp.exp(t - top[0]) for t in top]
    inv = 1.0 / (ex[0] + ex[1] + ex[2] + ex[3])
    gate = jnp.zeros(logits.shape, F32)
    for kk in range(TOP_K):
        gate = jnp.where(lane == float(kk), ex[kk] * inv, gate)
    e_ref[...] = e_out.astype(I32)
    gate_ref[...] = gate


def _mix(att, hm, og, h, prm, tm, alpha):
    rows, d = h.shape
    d_att = att.shape[-1]
    d_mv = hm.shape[-1]
    row = lambda w: pl.BlockSpec((tm, w), lambda i: (i, 0))
    const = lambda r, c: pl.BlockSpec((r, c), lambda i: (0, 0), pipeline_mode=pl.Buffered(1))
    return pl.pallas_call(
        functools.partial(_mix_kernel, alpha=alpha),
        grid=(rows // tm,),
        in_specs=[row(d_att), row(d_mv), row(d_mv), row(d), const(1, d_mv), const(d_att + d_mv, d),
                  const(1, d), const(1, d), const(d, LANES), const(d, LANES), const(1, LANES)],
        out_specs=(row(d), row(d // 2), row(LANES), row(LANES)),
        out_shape=(jax.ShapeDtypeStruct((rows, d), F32), jax.ShapeDtypeStruct((rows, d // 2), jnp.uint32),
                   jax.ShapeDtypeStruct((rows, LANES), I32), jax.ShapeDtypeStruct((rows, LANES), F32)),
        compiler_params=_cparams("parallel"),
        name="mix",
    )(att, hm, og, h, prm["g_m_out"], prm["w_out"], prm["ln1_g"], prm["ln1_b"], prm["wr_hi"], prm["wr_lo"],
      prm["b_router"])


def _moe_kernel(ie_ref, ns_ref, iv_ref, st_ref, sp_ref,
                x_hbm, wg_ref, wu_ref, bg_ref, bu_ref, wd_ref, bd_ref,
                y_hbm, xbuf, xb, acc, gsem, ssem, *, n_tok, n_items, n_ft):
    i = pl.program_id(0)
    f = pl.program_id(1)
    nf = pl.num_programs(1)
    slot = i % 2
    other = 1 - slot
    n_sub = ns_ref[i]
    valid = iv_ref[i] == 1
    half = xbuf.shape[-1]
    dump_base = TOP_K * n_tok

    def gather_rows(item, sl, row0, count):
        base = st_ref[item] + row0
        dst = xbuf.at[sl, pl.ds(row0, count), :]
        for r in range(count):
            pair = sp_ref[base + r]
            tok = jnp.maximum(pair, 0) >> LOG2_TOP_K
            pltpu.make_async_copy(x_hbm.at[pl.ds(tok, 1), :], dst.at[pl.ds(r, 1), :], gsem.at[sl]).start()

    def scatter_rows(item, sl, row0, count):
        base = st_ref[item] + row0
        n_real = ns_ref[item] * MOE_SUB - row0
        dump = dump_base + sl * MOE_ROWS + row0
        src = acc.at[sl, pl.ds(row0, count), :]
        for r in range(count):
            pair = sp_ref[base + r]
            real = jnp.logical_and(pair >= 0, r < n_real)
            dst = jnp.where(real, (pair & (TOP_K - 1)) * n_tok + (pair >> LOG2_TOP_K), dump + r)
            pltpu.make_async_copy(src.at[pl.ds(r, 1), :], y_hbm.at[pl.ds(dst, 1), :], ssem.at[sl]).start()

    def gather_wait(sl):
        pltpu.make_async_copy(x_hbm.at[pl.ds(0, MOE_ROWS), :], xbuf.at[sl], gsem.at[sl]).wait()

    def scatter_wait(sl):
        pltpu.make_async_copy(acc.at[sl], y_hbm.at[pl.ds(0, MOE_ROWS), :], ssem.at[sl]).wait()

    def all_steps(fn):
        def body(step, carry):
            fn(pl.multiple_of(step * MOE_SUB, MOE_SUB))
            return carry
        lax.fori_loop(0, MOE_ROWS // MOE_SUB, body, 0)

    nxt = jnp.minimum(i + 1, n_items - 1)
    has_next = jnp.logical_and(i + 1 < n_items, iv_ref[nxt] == 1)
    prv = jnp.maximum(i - 1, 0)
    has_prev = jnp.logical_and(i >= 1, iv_ref[prv] == 1)
    prv2 = jnp.maximum(i - 2, 0)
    has_prev2 = jnp.logical_and(i >= 2, iv_ref[prv2] == 1)

    @pl.when(f == 0)
    def _():
        @pl.when(jnp.logical_and(i == 0, valid))
        def _():
            all_steps(lambda row0: gather_rows(0, 0, row0, MOE_SUB))

        @pl.when(valid)
        def _():
            gather_wait(slot)
            packed = xbuf[slot]
            hi = lax.bitcast_convert_type(packed & jnp.uint32(0xFFFF0000), F32)
            lo = lax.bitcast_convert_type(packed << 16, F32)
            xb[:, 0:half] = hi.astype(BF16)
            xb[:, half:2 * half] = lo.astype(BF16)

        @pl.when(has_prev2)
        def _():
            scatter_wait(slot)

        @pl.when(valid)
        def _():
            acc[slot] = jnp.broadcast_to(bd_ref[0], acc.shape[1:])

    per_step = MOE_ROWS // n_ft
    row0 = pl.multiple_of(f * per_step, per_step)
    per_block = MOE_BLOCK // MOE_SUB
    n_full = n_sub // per_block
    max_full = MOE_ROWS // MOE_BLOCK
    share = per_step // max_full
    fused = jnp.logical_and(jnp.logical_and(valid, n_full == max_full), jnp.logical_and(has_next, has_prev))

    @pl.when(jnp.logical_and(has_next, jnp.logical_not(fused)))
    def _():
        gather_rows(i + 1, other, row0, per_step)

    @pl.when(jnp.logical_and(has_prev, jnp.logical_not(fused)))
    def _():
        scatter_rows(i - 1, other, row0, per_step)

    @pl.when(valid)
    def _():
        wg = wg_ref[0].astype(BF16)
        wu = wu_ref[0].astype(BF16)
        wd = wd_ref[0].astype(BF16)

        def rows_block(start, count):
            rows = pl.ds(start, count)
            xs = xb[rows, :]
            gp = _dot(xs, wg) + bg_ref[0]
            up = _dot(xs, wu) + bu_ref[0]
            gp = jnp.minimum(gp, SWIGLU_LIMIT)
            up = jnp.clip(up, -SWIGLU_LIMIT, SWIGLU_LIMIT)
            act = gp * _sigmoid(SWIGLU_ALPHA * gp) * (up + 1.0)
            acc[slot, rows, :] += _dot(act.astype(BF16), wd)

        def body(blk, carry):
            rows_block(pl.multiple_of(blk * MOE_BLOCK, MOE_BLOCK), MOE_BLOCK)
            return carry

        def body_fused(blk, carry):
            r0 = pl.multiple_of(row0 + blk * share, 8)
            gather_rows(i + 1, other, r0, share)
            scatter_rows(i - 1, other, r0, share)
            rows_block(pl.multiple_of(blk * MOE_BLOCK, MOE_BLOCK), MOE_BLOCK)
            return carry

        @pl.when(fused)
        def _():
            lax.fori_loop(0, max_full, body_fused, 0)

        @pl.when(jnp.logical_not(fused))
        def _():
            lax.fori_loop(0, n_full, body, 0)

        tail = n_sub - n_full * per_block
        for nn in range(1, per_block):
            @pl.when(tail == nn)
            def _():
                rows_block(pl.multiple_of(n_full * MOE_BLOCK, MOE_BLOCK), nn * MOE_SUB)

    @pl.when(jnp.logical_and(i == n_items - 1, f == nf - 1))
    def _():
        @pl.when(valid)
        def _():
            all_steps(lambda row0: scatter_rows(i, slot, row0, MOE_SUB))
            scatter_wait(slot)

        @pl.when(has_prev)
        def _():
            scatter_wait(other)


def _moe(x_packed, item_e, item_ns, item_valid, item_start, slot_pair, w_gate_up, b_gate_up, w_down, b_down):
    n_tok, half = x_packed.shape
    d = 2 * half
    n_exp, _, two_f = w_gate_up.shape
    nf = two_f // 2 // MOE_FT
    n_items = item_e.shape[0]
    assert MOE_ROWS % nf == 0 and (MOE_ROWS // nf) % (8 * (MOE_ROWS // MOE_BLOCK)) == 0

    def ftile(i, f, iv):
        return f * iv[i] + (nf - 1) * (1 - iv[i])

    grid_spec = pltpu.PrefetchScalarGridSpec(
        num_scalar_prefetch=5,
        grid=(n_items, nf),
        in_specs=[
            pl.BlockSpec(memory_space=pl.ANY),
            pl.BlockSpec((1, d, MOE_FT), lambda i, f, ie, ns, iv, st, sp: (ie[i], 0, ftile(i, f, iv))),
            pl.BlockSpec((1, d, MOE_FT), lambda i, f, ie, ns, iv, st, sp: (ie[i], 0, nf + ftile(i, f, iv))),
            pl.BlockSpec((1, 1, MOE_FT), lambda i, f, ie, ns, iv, st, sp: (ie[i], 0, ftile(i, f, iv))),
            pl.BlockSpec((1, 1, MOE_FT), lambda i, f, ie, ns, iv, st, sp: (ie[i], 0, nf + ftile(i, f, iv))),
            pl.BlockSpec((1, MOE_FT, d), lambda i, f, ie, ns, iv, st, sp: (ie[i], ftile(i, f, iv), 0)),
            pl.BlockSpec((1, 1, d), lambda i, f, ie, ns, iv, st, sp: (ie[i], 0, 0)),
        ],
        out_specs=pl.BlockSpec(memory_space=pl.ANY),
        scratch_shapes=[pltpu.VMEM((2, MOE_ROWS, half), jnp.uint32), pltpu.VMEM((MOE_ROWS, d), BF16),
                        pltpu.VMEM((2, MOE_ROWS, d), F32),
                        pltpu.SemaphoreType.DMA((2,)), pltpu.SemaphoreType.DMA((2,))],
    )
    return pl.pallas_call(
        functools.partial(_moe_kernel, n_tok=n_tok, n_items=n_items, n_ft=nf),
        grid_spec=grid_spec,
        out_shape=jax.ShapeDtypeStruct((TOP_K * n_tok + 2 * MOE_ROWS, d), F32),
        compiler_params=_cparams("arbitrary", "arbitrary"),
        name="moe",
    )(item_e, item_ns, item_valid, item_start, slot_pair, x_packed, w_gate_up, w_gate_up,
      b_gate_up.reshape(n_exp, 1, two_f), b_gate_up.reshape(n_exp, 1, two_f), w_down, b_down.reshape(n_exp, 1, d))


def _combine_kernel(y0_ref, y1_ref, y2_ref, y3_ref, gate_ref, h1_ref, lg_ref, lb_ref, o_ref, *, alpha):
    acc = alpha * h1_ref[...]
    for kk, y_ref in enumerate((y0_ref, y1_ref, y2_ref, y3_ref)):
        acc = acc + y_ref[...] * gate_ref[:, kk:kk + 1]
    o_ref[0] = _layer_norm(acc, lg_ref[...], lb_ref[...])


def _combine(y_pairs, gate, h1, ln_g, ln_b, alpha, *, n_tok, tok0, group_rows, skip, out_rows, tm):
    d = h1.shape[-1]
    groups = h1.shape[0] // group_rows
    assert out_rows % tm == 0 and skip % 8 == 0 and group_rows % 8 == 0 and tok0 % 8 == 0 and n_tok % 8 == 0

    def rows(base):
        return lambda b, j: (pl.multiple_of(base + b * group_rows + skip + j * tm, 8), 0)

    def el(w, base):
        return pl.BlockSpec((pl.Element(tm), pl.Element(w)), rows(base))

    const = lambda c: pl.BlockSpec((1, c), lambda b, j: (0, 0))
    y_specs = [el(d, kk * n_tok + tok0) for kk in range(TOP_K)]
    return pl.pallas_call(
        functools.partial(_combine_kernel, alpha=alpha),
        grid=(groups, out_rows // tm),
        in_specs=y_specs + [el(LANES, 0), el(d, 0), const(d), const(d)],
        out_specs=pl.BlockSpec((1, tm, d), lambda b, j: (b, j, 0)),
        out_shape=jax.ShapeDtypeStruct((groups, out_rows, d), F32),
        compiler_params=_cparams("parallel", "parallel"),
        name="combine",
    )(*([y_pairs] * TOP_K), gate, h1, ln_g, ln_b)


def _route(top_e):
    n_pairs = top_e.shape[0] * TOP_K
    per_item = MOE_ROWS // MOE_SUB
    pair_e = top_e.reshape(-1)
    order = jnp.argsort(pair_e, stable=True).astype(I32)
    e_sorted = pair_e[order]
    edges = jnp.arange(N_EXPERTS + 1, dtype=I32)
    bounds = jnp.sum((e_sorted[None, :] < edges[:, None]).astype(I32), axis=1)
    start = bounds[:-1]
    counts = bounds[1:] - start
    n_sb = (counts + MOE_SUB - 1) // MOE_SUB
    sb_end = jnp.cumsum(n_sb)
    sb_start = sb_end - n_sb
    max_sb = n_pairs // MOE_SUB + N_EXPERTS
    slot = jnp.arange(max_sb * MOE_SUB, dtype=I32)
    e_slot = jnp.minimum(jnp.sum((slot[:, None] >= (sb_end * MOE_SUB)[None, :]).astype(I32), axis=1), N_EXPERTS - 1)
    rank = slot - sb_start[e_slot] * MOE_SUB
    src = jnp.clip(start[e_slot] + rank, 0, n_pairs - 1)
    slot_pair = jnp.where(rank < counts[e_slot], order[src], -1).astype(I32)
    slot_pair = jnp.concatenate([slot_pair, jnp.full((MOE_ROWS,), -1, I32)])

    n_it = (n_sb + per_item - 1) // per_item
    it_end = jnp.cumsum(n_it)
    it_start = it_end - n_it
    max_items = (max_sb + (per_item - 1) * N_EXPERTS) // per_item
    idx = jnp.arange(max_items, dtype=I32)
    n_valid = it_end[-1]
    item_valid = (idx < n_valid).astype(I32)
    idc = jnp.minimum(idx, n_valid - 1)
    item_e = jnp.minimum(jnp.sum((idc[:, None] >= it_end[None, :]).astype(I32), axis=1), N_EXPERTS - 1).astype(I32)
    chunk = idc - it_start[item_e]
    item_start = ((sb_start[item_e] + per_item * chunk) * MOE_SUB).astype(I32)
    item_ns = (jnp.clip(n_sb[item_e] - per_item * chunk, 0, per_item) * item_valid).astype(I32)
    return item_e, item_ns, item_valid, item_start, slot_pair


def _prep_params(ln_in_g, ln_in_b, w_in, b_att_f, g_att_q, g_att_k, b_m_i, b_m_f, g_m_out, w_out,
                 ln1_g, ln1_b, w_router, b_router, ln2_g, ln2_b):
    d = w_in.shape[0]
    d_att = N_ATT_HEADS * ATT_HEAD_DIM
    d_mqk = N_MLSTM_HEADS * MLSTM_QK_DIM
    d_mv = N_MLSTM_HEADS * MLSTM_V_DIM
    sizes = (d_att, d_att, d_att, N_ATT_HEADS, d_mqk, d_mqk, d_mv, N_MLSTM_HEADS, N_MLSTM_HEADS, d_mv)
    offs = [0]
    for s in sizes:
        offs.append(offs[-1] + s)
    cols = lambda i: w_in[:, offs[i]:offs[i + 1]]
    w_main = jnp.concatenate([cols(0), cols(1), cols(2), cols(4), cols(5), cols(6), cols(9)], axis=1).astype(BF16)
    w_gate = jnp.concatenate([cols(3), cols(7), cols(8), jnp.zeros((d, LANES - N_GATES), F32)], axis=1)
    wg_hi, wg_lo = _split2(w_gate)
    gate_bias = jnp.concatenate([b_att_f, b_m_i, b_m_f, jnp.zeros((LANES - N_GATES,), F32)]).reshape(1, LANES)
    w_r = jnp.concatenate([w_router, jnp.zeros((d, LANES - N_EXPERTS), F32)], axis=1)
    wr_hi, wr_lo = _split2(w_r)
    b_r = jnp.concatenate([b_router, jnp.full((LANES - N_EXPERTS,), NEG, F32)]).reshape(1, LANES)
    return dict(
        ln_in_g=ln_in_g.reshape(1, d), ln_in_b=ln_in_b.reshape(1, d), w_main=w_main, wg_hi=wg_hi, wg_lo=wg_lo,
        gate_bias=gate_bias, g_att_q=g_att_q.reshape(1, -1), g_att_k=g_att_k.reshape(1, -1),
        g_m_out=g_m_out.reshape(1, -1), w_out=w_out.astype(BF16), ln1_g=ln1_g.reshape(1, d),
        ln1_b=ln1_b.reshape(1, d), wr_hi=wr_hi, wr_lo=wr_lo, b_router=b_r,
        ln2_g=ln2_g.reshape(1, d), ln2_b=ln2_b.reshape(1, d))


def kernel(x_prompt, x_sample, cache_k, cache_v, cache_logf, state_C, state_n, state_m, page_table,
           meta_tokens, ln_in_g, ln_in_b, w_in, b_att_f, g_att_q, g_att_k, b_m_i, b_m_f, g_m_out, w_out,
           ln1_g, ln1_b, w_router, b_router, w_gate_up, b_gate_up, w_down, b_down, ln2_g, ln2_b):
    depth = w_in.shape[0]
    assert depth == 1, "single-layer trunk only"
    nb, seq, d = x_prompt.shape
    db = x_sample.shape[0]
    assert x_sample.shape[1] == 1, "one new token per sample"
    alpha = (2.0 * depth) ** 0.25
    d_att = N_ATT_HEADS * ATT_HEAD_DIM
    hm = N_MLSTM_HEADS
    prm = _prep_params(ln_in_g, ln_in_b, w_in[0], b_att_f[0], g_att_q[0], g_att_k[0], b_m_i[0], b_m_f[0],
                       g_m_out[0], w_out[0], ln1_g[0], ln1_b[0], w_router[0], b_router[0], ln2_g[0], ln2_b[0])

    length = N_META + seq
    lp = pl.cdiv(length, CHUNK) * CHUNK
    meta = jnp.broadcast_to(meta_tokens[None], (nb, N_META, d))
    xp = jnp.pad(x_prompt, ((0, 0), (N_META, lp - length), (0, 0)))
    xp = lax.dynamic_update_slice(xp, meta, (0, 0, 0))
    (hp, qa, k32, kb, v32, vb, gates, qm, km, vm, og) = _proj(xp, length, lp // 8, prm)
    cum_l, cum_lt, gates_t, k_aug, q_aug = _gate_scan(gates)
    att_p = _fox_prompt(qa, q_aug, kb, k_aug, vb)
    hm_p, c_p, n_p, m_p = _mlstm_prompt(qm, km, vm, gates, cum_l, gates_t, cum_lt)
    rows_p = nb * lp
    h1_p, h1p_p, e_p, gate_p = _mix(att_p.reshape(rows_p, d_att), hm_p.reshape(rows_p, -1), og.reshape(rows_p, -1),
                                    hp.reshape(rows_p, d), prm, 256, alpha)

    xs = x_sample.reshape(1, db, d)
    (hs, qa_s, k32_s, _, v32_s, _, gates_s, qm_s, km_s, vm_s, og_s) = _proj(xs, db, db, prm)
    n_pool, page = cache_k.shape[1], cache_k.shape[2]
    nh, dh = N_ATT_HEADS, ATT_HEAD_DIM
    lf_s = gates_s[0, :, GATE_LF_ATT:GATE_LF_ATT + nh]
    att_s = _paged_attention(
        qa_s.astype(F32).reshape(db, nh, dh), k32_s.reshape(db, nh, dh), v32_s.reshape(db, nh, dh),
        jnp.tile(lf_s, (1, page)).reshape(db, 1, page * nh), cache_k, cache_v,
        cache_logf[0].reshape(n_pool, 1, page * nh), page_table)
    ig_s = gates_s[0, :, GATE_IG:GATE_IG + hm].reshape(db, hm, 1, 1)
    lfm_s = gates_s[0, :, GATE_LF_M:GATE_LF_M + hm].reshape(db, hm, 1, 1)
    hm_s, c_s, n_s, m_s = _mlstm_step(
        qm_s.astype(F32).reshape(db, hm, MLSTM_QK_DIM, 1), km_s.astype(F32).reshape(db, hm, MLSTM_QK_DIM, 1),
        vm_s.astype(F32).reshape(db, hm, 1, MLSTM_V_DIM), ig_s, lfm_s, state_C[0],
        state_n[0].reshape(db, hm, MLSTM_QK_DIM, 1), state_m[0].reshape(db, hm, 1, 1))
    h1_s, h1p_s, e_s, gate_s = _mix(att_s.reshape(db, d_att), hm_s.reshape(db, -1), og_s.reshape(db, -1),
                                    hs.reshape(db, d), prm, db, alpha)

    n_tok = rows_p + db
    x_packed = jnp.concatenate([h1p_p, h1p_s], axis=0)
    top_e = jnp.concatenate([e_p, e_s], axis=0)[:, :TOP_K]
    item_e, item_ns, item_valid, item_start, slot_pair = _route(top_e)
    y_pairs = _moe(x_packed, item_e, item_ns, item_valid, item_start, slot_pair,
                   w_gate_up[0], b_gate_up[0], w_down[0], b_down[0])
    y_prompt = _combine(y_pairs, gate_p, h1_p, prm["ln2_g"], prm["ln2_b"], alpha, n_tok=n_tok, tok0=0,
                        group_rows=lp, skip=N_META, out_rows=seq, tm=256)
    y_sample = _combine(y_pairs, gate_s, h1_s, prm["ln2_g"], prm["ln2_b"], alpha, n_tok=n_tok, tok0=rows_p,
                        group_rows=db, skip=0, out_rows=db, tm=db).reshape(db, 1, d)
    k_prompt = k32.reshape(1, nb, length, N_ATT_HEADS, ATT_HEAD_DIM)
    v_prompt = v32.reshape(1, nb, length, N_ATT_HEADS, ATT_HEAD_DIM)
    lf_prompt = gates[:, :length, GATE_LF_ATT:GATE_LF_ATT + N_ATT_HEADS][None]
    return (y_prompt, y_sample, k_prompt, v_prompt, lf_prompt,
            c_p[None], n_p.reshape(1, nb, hm, MLSTM_QK_DIM), m_p[:, :, 0, 0][None],
            k32_s.reshape(1, db, 1, N_ATT_HEADS, ATT_HEAD_DIM), v32_s.reshape(1, db, 1, N_ATT_HEADS, ATT_HEAD_DIM),
            lf_s.reshape(1, db, 1, nh),
            c_s[None], n_s.reshape(1, db, hm, MLSTM_QK_DIM), m_s.reshape(1, db, hm))
```

```python
import functools

import jax
import jax.numpy as jnp
from jax import lax
from jax.experimental import pallas as pl
from jax.experimental.pallas import tpu as pltpu

F32 = jnp.float32
BF16 = jnp.bfloat16
I32 = jnp.int32

N_META = 16
CHUNK = 128
ATT_HEAD_DIM = 128
N_ATT_HEADS = 8
N_MLSTM_HEADS = 4
MLSTM_QK_DIM = 128
MLSTM_V_DIM = 256
N_EXPERTS = 32
TOP_K = 4
LOG2_TOP_K = 2
SWIGLU_ALPHA = 1.702
SWIGLU_LIMIT = 7.0
LN_EPS = 1e-5
RMS_EPS = 1e-6
NEG = -1e30

LANES = 128
GATE_LF_ATT = 0
GATE_IG = N_ATT_HEADS
GATE_LF_M = GATE_IG + N_MLSTM_HEADS
N_GATES = GATE_LF_M + N_MLSTM_HEADS

VMEM_LIMIT = 56 * 1024 * 1024

MOE_ROWS = 1152
MOE_SUB = 128
MOE_FT = 256
MOE_BLOCK = 512
PAGES_PER_STEP = 8
FOX_TQ = 256


def _cparams(*sem):
    return pltpu.CompilerParams(dimension_semantics=sem, vmem_limit_bytes=VMEM_LIMIT)


def _split2(x):
    hi = x.astype(BF16)
    lo = (x - hi.astype(F32)).astype(BF16)
    return hi, lo


def _split3(x):
    hi = x.astype(BF16)
    r1 = x - hi.astype(F32)
    mid = r1.astype(BF16)
    lo = (r1 - mid.astype(F32)).astype(BF16)
    return hi, mid, lo


def _dot(a, b):
    return jnp.dot(a, b, preferred_element_type=F32)


def _dot_nt(a, b):
    return lax.dot_general(a, b, (((1,), (1,)), ((), ())), preferred_element_type=F32)


def _dot_tn(a, b):
    return lax.dot_general(a, b, (((0,), (0,)), ((), ())), preferred_element_type=F32)


def _dot3(x_f32, w_hi, w_lo):
    x_hi, x_lo = _split2(x_f32)
    return _dot(x_hi, w_hi) + (_dot(x_lo, w_hi) + _dot(x_hi, w_lo))


def _layer_norm(x, g, b):
    mu = jnp.mean(x, axis=-1, keepdims=True)
    xc = x - mu
    var = jnp.mean(xc * xc, axis=-1, keepdims=True)
    return xc * lax.rsqrt(var + LN_EPS) * g + b


def _log_sigmoid(z):
    return jnp.minimum(z, 0.0) - jnp.log(1.0 + jnp.exp(-jnp.abs(z)))


def _sigmoid(z):
    return 1.0 / (1.0 + jnp.exp(-z))


def _proj_kernel(x_ref, lng_ref, lnb_ref, w_ref, wgh_ref, wgl_ref, gb_ref, gq_ref, gk_ref,
                 h_ref, q_ref, k32_ref, kb_ref, v32_ref, vb_ref, g_ref, qm_ref, km_ref, vm_ref, og_ref,
                 *, tm, valid_len):
    d_att = N_ATT_HEADS * ATT_HEAD_DIM
    d_mqk = N_MLSTM_HEADS * MLSTM_QK_DIM
    d_mv = N_MLSTM_HEADS * MLSTM_V_DIM
    h = _layer_norm(x_ref[0], lng_ref[...], lnb_ref[...])
    h_ref[0] = h
    hb = h.astype(BF16)

    def seg(c0, width):
        return _dot(hb, w_ref[:, c0:c0 + width])

    scale = ATT_HEAD_DIM ** -0.5
    qa = seg(0, d_att)
    ka = seg(d_att, d_att)
    for hh in range(N_ATT_HEADS):
        sl = slice(hh * ATT_HEAD_DIM, (hh + 1) * ATT_HEAD_DIM)
        qh = qa[:, sl]
        qn = qh * lax.rsqrt(jnp.mean(qh * qh, axis=-1, keepdims=True) + RMS_EPS) * gq_ref[...]
        q_ref[0, :, sl] = (qn * scale).astype(BF16)
        kh = ka[:, sl]
        kn = kh * lax.rsqrt(jnp.mean(kh * kh, axis=-1, keepdims=True) + RMS_EPS) * gk_ref[...]
        k32_ref[0, :, sl] = kn
        kb_ref[0, :, sl] = kn.astype(BF16)
    va = seg(2 * d_att, d_att)
    v32_ref[0] = va
    vb_ref[0] = va.astype(BF16)
    c0 = 3 * d_att
    qm_ref[0] = seg(c0, d_mqk).astype(BF16)
    km_ref[0] = (seg(c0 + d_mqk, d_mqk) * (MLSTM_QK_DIM ** -0.5)).astype(BF16)
    vm_ref[0] = seg(c0 + 2 * d_mqk, d_mv).astype(BF16)
    og_ref[0] = _sigmoid(seg(c0 + 2 * d_mqk + d_mv, d_mv))

    z = _dot3(h, wgh_ref[...], wgl_ref[...]) + gb_ref[...]
    lane = lax.broadcasted_iota(I32, z.shape, 1)
    row = pl.program_id(1) * tm + lax.broadcasted_iota(I32, z.shape, 0)
    is_ig = jnp.logical_and(lane >= GATE_IG, lane < GATE_LF_M)
    gates = jnp.where(is_ig, z, jnp.where(lane < N_GATES, _log_sigmoid(z), 0.0))
    gates = jnp.where(row < valid_len, gates, jnp.where(is_ig, NEG, 0.0))
    g_ref[0] = gates


def _proj(x, valid_len, tm, prm):
    nb, lp, d = x.shape
    d_att = N_ATT_HEADS * ATT_HEAD_DIM
    d_mqk = N_MLSTM_HEADS * MLSTM_QK_DIM
    d_mv = N_MLSTM_HEADS * MLSTM_V_DIM
    n_main = prm["w_main"].shape[1]
    grid = (nb, lp // tm)
    row = lambda w: pl.BlockSpec((1, tm, w), lambda b, i: (b, i, 0))
    const = lambda r, c: pl.BlockSpec((r, c), lambda b, i: (0, 0), pipeline_mode=pl.Buffered(1))
    out_shape = (
        jax.ShapeDtypeStruct((nb, lp, d), F32),
        jax.ShapeDtypeStruct((nb, lp, d_att), BF16),
        jax.ShapeDtypeStruct((nb, valid_len, d_att), F32),
        jax.ShapeDtypeStruct((nb, lp, d_att), BF16),
        jax.ShapeDtypeStruct((nb, valid_len, d_att), F32),
        jax.ShapeDtypeStruct((nb, lp, d_att), BF16),
        jax.ShapeDtypeStruct((nb, lp, LANES), F32),
        jax.ShapeDtypeStruct((nb, lp, d_mqk), BF16),
        jax.ShapeDtypeStruct((nb, lp, d_mqk), BF16),
        jax.ShapeDtypeStruct((nb, lp, d_mv), BF16),
        jax.ShapeDtypeStruct((nb, lp, d_mv), F32),
    )
    out_specs = (row(d), row(d_att), row(d_att), row(d_att), row(d_att), row(d_att), row(LANES),
                 row(d_mqk), row(d_mqk), row(d_mv), row(d_mv))
    return pl.pallas_call(
        functools.partial(_proj_kernel, tm=tm, valid_len=valid_len),
        grid=grid,
        in_specs=[row(d), const(1, d), const(1, d), const(d, n_main), const(d, LANES), const(d, LANES),
                  const(1, LANES), const(1, ATT_HEAD_DIM), const(1, ATT_HEAD_DIM)],
        out_specs=out_specs,
        out_shape=out_shape,
        compiler_params=_cparams("parallel", "arbitrary"),
        name="proj",
    )(x, prm["ln_in_g"], prm["ln_in_b"], prm["w_main"], prm["wg_hi"], prm["wg_lo"], prm["gate_bias"],
      prm["g_att_q"], prm["g_att_k"])


def _gate_scan_kernel(g_ref, cl_ref, clt_ref, gt_ref, ka_ref, qa_ref, *, n_chunks):
    r = lax.broadcasted_iota(I32, (CHUNK, CHUNK), 0)
    c = lax.broadcasted_iota(I32, (CHUNK, CHUNK), 1)
    tri = jnp.where(c <= r, 1.0, 0.0).astype(BF16)
    carry = jnp.zeros((1, LANES), F32)
    for ci in range(n_chunks):
        sl = slice(ci * CHUNK, (ci + 1) * CHUNK)
        g = g_ref[0, sl, :]
        hi, mid, lo = _split3(g)
        loc = _dot(tri, hi) + (_dot(tri, mid) + _dot(tri, lo))
        glob = loc + carry
        carry = glob[CHUNK - 1:CHUNK, :]
        cl_ref[0, sl, :] = loc
        clt_ref[0, ci] = loc.T
        gt_ref[0, ci] = g.T
        for hh in range(N_ATT_HEADS):
            cum = glob[:, GATE_LF_ATT + hh:GATE_LF_ATT + hh + 1]
            t0 = cum.astype(BF16).astype(F32)
            r1 = cum - t0
            t1 = r1.astype(BF16).astype(F32)
            t2 = (r1 - t1).astype(BF16).astype(F32)
            hs = slice(hh * ATT_HEAD_DIM, (hh + 1) * ATT_HEAD_DIM)
            ones = jnp.where(c < 3, 1.0, 0.0)
            terms = jnp.where(c == 3, t0, jnp.where(c == 4, t1, jnp.where(c == 5, t2, 0.0)))
            qa_ref[0, sl, hs] = (ones + terms).astype(BF16)
            ones_k = jnp.where(jnp.logical_and(c >= 3, c < 6), 1.0, 0.0)
            terms_k = jnp.where(c == 0, t0, jnp.where(c == 1, t1, jnp.where(c == 2, t2, 0.0)))
            ka_ref[0, sl, hs] = (ones_k - terms_k).astype(BF16)


def _gate_scan(gates):
    nb, lp, _ = gates.shape
    nc = lp // CHUNK
    d_att = N_ATT_HEADS * ATT_HEAD_DIM
    col = pl.BlockSpec((1, lp, LANES), lambda b: (b, 0, 0))
    rowt = pl.BlockSpec((1, nc, LANES, CHUNK), lambda b: (b, 0, 0, 0))
    aug = pl.BlockSpec((1, lp, d_att), lambda b: (b, 0, 0))
    return pl.pallas_call(
        functools.partial(_gate_scan_kernel, n_chunks=nc),
        grid=(nb,),
        in_specs=[col],
        out_specs=(col, rowt, rowt, aug, aug),
        out_shape=(jax.ShapeDtypeStruct((nb, lp, LANES), F32),
                   jax.ShapeDtypeStruct((nb, nc, LANES, CHUNK), F32),
                   jax.ShapeDtypeStruct((nb, nc, LANES, CHUNK), F32),
                   jax.ShapeDtypeStruct((nb, lp, d_att), BF16),
                   jax.ShapeDtypeStruct((nb, lp, d_att), BF16)),
        compiler_params=_cparams("parallel"),
        name="gate_scan",
    )(gates)


def _fox_kernel(q_ref, qa_ref, k_ref, ka_ref, v_ref, o_ref, kx_sc, vt_sc, qx_sc, m_sc, l_sc, acc_sc,
                *, n_chunks, tq):
    ti = pl.program_id(1)
    dh = ATT_HEAD_DIM

    @pl.when(ti == 0)
    def _():
        for j in range(n_chunks):
            rows = slice(j * CHUNK, (j + 1) * CHUNK)
            for hh in range(N_ATT_HEADS):
                sl = slice(hh * dh, (hh + 1) * dh)
                kx_sc[j, :, 2 * hh * dh:(2 * hh + 1) * dh] = k_ref[0, rows, sl]
                kx_sc[j, :, (2 * hh + 1) * dh:(2 * hh + 2) * dh] = ka_ref[0, rows, sl]
                vt_sc[hh, j] = v_ref[0, rows, sl].astype(F32).T.astype(BF16)

    for hh in range(N_ATT_HEADS):
        sl = slice(hh * dh, (hh + 1) * dh)
        qx_sc[:, 2 * hh * dh:(2 * hh + 1) * dh] = q_ref[0, :, sl]
        qx_sc[:, (2 * hh + 1) * dh:(2 * hh + 2) * dh] = qa_ref[0, :, sl]
    m_sc[...] = jnp.full(m_sc.shape, NEG, F32)
    l_sc[...] = jnp.zeros(l_sc.shape, F32)
    acc_sc[...] = jnp.zeros(acc_sc.shape, F32)
    key = lax.broadcasted_iota(I32, (CHUNK, tq), 0)
    qry = lax.broadcasted_iota(I32, (CHUNK, tq), 1)

    def step(j, masked):
        scores = []
        for hh in range(N_ATT_HEADS):
            xs = slice(2 * hh * dh, (2 * hh + 2) * dh)
            scores.append(_dot_nt(kx_sc[j, :, xs], qx_sc[:, xs]))
        probs, alphas = [], []
        for hh in range(N_ATT_HEADS):
            s = scores[hh]
            if masked:
                s = jnp.where(j * CHUNK + key <= ti * tq + qry, s, NEG)
            m_old = m_sc[hh]
            m_new = jnp.maximum(m_old, jnp.max(s, axis=0, keepdims=True))
            p = jnp.exp(s - m_new)
            alpha = jnp.exp(m_old - m_new)
            l_sc[hh] = alpha * l_sc[hh] + jnp.sum(p, axis=0, keepdims=True)
            m_sc[hh] = m_new
            probs.append(p.astype(BF16))
            alphas.append(alpha)
        for hh in range(N_ATT_HEADS):
            acc_sc[hh] = alphas[hh] * acc_sc[hh] + _dot(vt_sc[hh, j], probs[hh])

    def body(j, carry):
        step(j, False)
        return carry

    per = tq // CHUNK
    lax.fori_loop(0, ti * per, body, 0)
    for dj in range(per):
        j = ti * per + dj

        @pl.when(j < n_chunks)
        def _():
            step(j, True)

    for hh in range(N_ATT_HEADS):
        sl = slice(hh * dh, (hh + 1) * dh)
        o_ref[0, :, sl] = (acc_sc[hh] / l_sc[hh]).T.astype(o_ref.dtype)


def _fox_prompt(q, q_aug, kb, k_aug, vb):
    nb, lp, d_att = q.shape
    nc = lp // CHUNK
    tq = FOX_TQ
    blk = pl.BlockSpec((1, tq, d_att), lambda b, i: (b, i, 0))
    full = pl.BlockSpec((1, lp, d_att), lambda b, i: (b, 0, 0))
    nh, dh = N_ATT_HEADS, ATT_HEAD_DIM
    return pl.pallas_call(
        functools.partial(_fox_kernel, n_chunks=nc, tq=tq),
        grid=(nb, pl.cdiv(lp, tq)),
        in_specs=[blk, blk, full, full, full],
        out_specs=blk,
        out_shape=jax.ShapeDtypeStruct((nb, lp, d_att), BF16),
        scratch_shapes=[pltpu.VMEM((nc, CHUNK, 2 * d_att), BF16),
                        pltpu.VMEM((nh, nc, dh, CHUNK), BF16),
                        pltpu.VMEM((tq, 2 * d_att), BF16),
                        pltpu.VMEM((nh, 1, tq), F32), pltpu.VMEM((nh, 1, tq), F32),
                        pltpu.VMEM((nh, dh, tq), F32)],
        compiler_params=_cparams("parallel", "arbitrary"),
        name="fox_prompt",
    )(q, q_aug, kb, k_aug, vb)


def _mlstm_kernel(q_ref, k_ref, v_ref, g_ref, cl_ref, gt_ref, clt_ref,
                  h_ref, c_out_ref, n_out_ref, m_out_ref, c_sc, n_sc, m_sc):
    ci = pl.program_id(1)

    @pl.when(ci == 0)
    def _():
        c_sc[...] = jnp.zeros(c_sc.shape, F32)
        n_sc[...] = jnp.zeros(n_sc.shape, F32)
        m_sc[...] = jnp.zeros(m_sc.shape, F32)

    r = lax.broadcasted_iota(I32, (CHUNK, CHUNK), 0)
    c = lax.broadcasted_iota(I32, (CHUNK, CHUNK), 1)
    causal = c <= r
    last = slice(CHUNK - 1, CHUNK)
    heads = range(N_MLSTM_HEADS)
    qk = [slice(hh * MLSTM_QK_DIM, (hh + 1) * MLSTM_QK_DIM) for hh in heads]
    vv = [slice(hh * MLSTM_V_DIM, (hh + 1) * MLSTM_V_DIM) for hh in heads]
    qh = [q_ref[0, :, qk[hh]] for hh in heads]
    kh = [k_ref[0, :, qk[hh]] for hh in heads]
    vh = [v_ref[0, :, vv[hh]] for hh in heads]
    c0 = [c_sc[hh] for hh in heads]
    qk_raw = [_dot_nt(qh[hh], kh[hh]) for hh in heads]
    q_c0 = [_dot(qh[hh], c0[hh].astype(BF16)) for hh in heads]

    s_bf, kw_bf, stash = [], [], []
    for hh in heads:
        ig_l, lf_l = GATE_IG + hh, GATE_LF_M + hh
        b_col = cl_ref[0, :, lf_l:lf_l + 1]
        ig_col = g_ref[0, :, ig_l:ig_l + 1]
        b_row = clt_ref[0, 0, lf_l:lf_l + 1, :]
        ig_row = gt_ref[0, 0, ig_l:ig_l + 1, :]
        m0 = m_sc[hh][:, 0:1]
        n0 = n_sc[hh]
        dmat = jnp.where(causal, b_col - b_row + ig_row, NEG)
        inter = b_col + m0
        mt = jnp.maximum(inter, jnp.max(dmat, axis=-1, keepdims=True))
        w_inter = jnp.exp(inter - mt)
        s = qk_raw[hh] * jnp.exp(dmat - mt)
        den = jnp.sum(s, axis=-1, keepdims=True) + w_inter * jnp.sum(qh[hh].astype(F32) * n0, axis=-1, keepdims=True)
        m_new = mt[last, :]
        b_last = b_col[last, :]
        w_state = jnp.exp(b_last - b_col + ig_col - m_new)
        decay = jnp.exp(b_last + m0 - m_new)
        kw = kh[hh].astype(F32) * w_state
        s_bf.append(s.astype(BF16))
        kw_bf.append(kw.astype(BF16))
        stash.append((w_inter, den, mt, m_new, decay, decay * n0 + jnp.sum(kw, axis=0, keepdims=True)))

    s_v = [_dot(s_bf[hh], vh[hh]) for hh in heads]
    kw_v = [_dot_tn(kw_bf[hh], vh[hh]) for hh in heads]
    for hh in heads:
        w_inter, den, mt, m_new, decay, n_new = stash[hh]
        num = s_v[hh] + w_inter * q_c0[hh]
        h_ref[0, :, vv[hh]] = num / jnp.maximum(jnp.abs(den), jnp.exp(-mt))
        c_new = decay * c0[hh] + kw_v[hh]
        c_sc[hh] = c_new
        n_sc[hh] = n_new
        m_sc[hh] = jnp.broadcast_to(m_new, (1, LANES))
        c_out_ref[0, hh] = c_new
        n_out_ref[0, hh] = n_new
        m_out_ref[0, hh] = jnp.broadcast_to(m_new, (1, LANES))


def _mlstm_prompt(qm, km, vm, gates, cum_l, gates_t, cum_lt):
    nb, lp, d_mqk = qm.shape
    d_mv = vm.shape[-1]
    nc = lp // CHUNK
    hm = N_MLSTM_HEADS
    blk = lambda w: pl.BlockSpec((1, CHUNK, w), lambda b, i: (b, i, 0))
    tblk = pl.BlockSpec((1, 1, LANES, CHUNK), lambda b, i: (b, i, 0, 0))
    return pl.pallas_call(
        _mlstm_kernel,
        grid=(nb, nc),
        in_specs=[blk(d_mqk), blk(d_mqk), blk(d_mv), blk(LANES), blk(LANES), tblk, tblk],
        out_specs=(blk(d_mv),
                   pl.BlockSpec((1, hm, MLSTM_QK_DIM, MLSTM_V_DIM), lambda b, i: (b, 0, 0, 0)),
                   pl.BlockSpec((1, hm, 1, MLSTM_QK_DIM), lambda b, i: (b, 0, 0, 0)),
                   pl.BlockSpec((1, hm, 1, LANES), lambda b, i: (b, 0, 0, 0))),
        out_shape=(jax.ShapeDtypeStruct((nb, lp, d_mv), F32),
                   jax.ShapeDtypeStruct((nb, hm, MLSTM_QK_DIM, MLSTM_V_DIM), F32),
                   jax.ShapeDtypeStruct((nb, hm, 1, MLSTM_QK_DIM), F32),
                   jax.ShapeDtypeStruct((nb, hm, 1, LANES), F32)),
        scratch_shapes=[pltpu.VMEM((hm, MLSTM_QK_DIM, MLSTM_V_DIM), F32),
                        pltpu.VMEM((hm, 1, MLSTM_QK_DIM), F32),
                        pltpu.VMEM((hm, 1, LANES), F32)],
        compiler_params=_cparams("parallel", "arbitrary"),
        name="mlstm_prompt",
    )(qm, km, vm, gates, cum_l, gates_t, cum_lt)


def _paged_kernel(pt_ref, q_ref, kn_ref, vn_ref, lfn_ref, *refs, pps):
    k_refs = refs[:pps]
    v_refs = refs[pps:2 * pps]
    lf_refs = refs[2 * pps:3 * pps]
    o_ref, lf_sc, bias_sc, m_sc, l_sc, acc_sc, carry_sc = refs[3 * pps:]
    g = pl.program_id(1)
    nh, dh = N_ATT_HEADS, ATT_HEAD_DIM
    page = k_refs[0].shape[0]
    width = page * nh
    q8 = q_ref[0]

    @pl.when(g == 0)
    def _():
        m_sc[...] = jnp.sum(q8 * kn_ref[0], axis=-1, keepdims=True)
        l_sc[...] = jnp.ones(l_sc.shape, F32)
        acc_sc[...] = vn_ref[0]
        carry_sc[...] = jnp.zeros(carry_sc.shape, F32)

    for u in range(pps):
        lf_sc[u:u + 1, :] = lf_refs[u][...]
    lf = lf_sc[...]
    lane = lax.broadcasted_iota(I32, (pps, width), 1)
    incl = lf
    k = nh
    while k < width:
        incl = incl + jnp.where(lane < width - k, pltpu.roll(incl, width - k, axis=1), 0.0)
        k *= 2
    tot = jnp.where(lane < nh, incl, 0.0)
    k = nh
    while k < width:
        tot = tot + pltpu.roll(tot, k, axis=1)
        k *= 2
    run = carry_sc[...]
    for u in range(pps):
        bias_sc[u:u + 1, :] = run
        run = run + tot[u:u + 1, :]
    carry_sc[...] = run
    bias = bias_sc[...] + (incl - lf) + lfn_ref[0]

    own = lax.broadcasted_iota(I32, (nh, width), 1) % nh == lax.broadcasted_iota(I32, (nh, width), 0)
    q_bf = q8.astype(BF16)
    scores = []
    m_old = m_sc[...]
    m_new = m_old
    for u in range(pps):
        kf = k_refs[u][...].reshape(width, dh).astype(BF16)
        s = jnp.where(own, _dot_nt(q_bf, kf) + bias[u:u + 1, :], NEG)
        m_new = jnp.maximum(m_new, jnp.max(s, axis=-1, keepdims=True))
        scores.append(s)
    alpha = jnp.exp(m_old - m_new)
    l_new = alpha * l_sc[...]
    acc = alpha * acc_sc[...]
    for u in range(pps):
        p = jnp.exp(scores[u] - m_new)
        l_new = l_new + jnp.sum(p, axis=-1, keepdims=True)
        acc = acc + _dot(p.astype(BF16), v_refs[u][...].reshape(width, dh).astype(BF16))
    m_sc[...] = m_new
    l_sc[...] = l_new
    acc_sc[...] = acc

    @pl.when(g == pl.num_programs(1) - 1)
    def _():
        o_ref[0] = acc / l_new


def _paged_attention(q, k_new, v_new, lf_new_flat, cache_k, cache_v, cache_lf_flat, page_table):
    db, n_pages = page_table.shape
    _, n_pool, page, nh, dh = cache_k.shape
    width = page * nh
    pps = PAGES_PER_STEP
    steps = n_pages // pps
    tok = lambda: pl.BlockSpec((1, nh, dh), lambda b, g, pt: (b, 0, 0))

    def page_id(b, g, pt, u):
        return pt[b * n_pages + (n_pages - 1 - (g * pps + u))]

    kv_specs = [pl.BlockSpec((None, None, page, nh, dh),
                             functools.partial(lambda b, g, pt, u: (0, page_id(b, g, pt, u), 0, 0, 0), u=u))
                for u in range(pps)]
    lf_specs = [pl.BlockSpec((None, 1, width),
                             functools.partial(lambda b, g, pt, u: (page_id(b, g, pt, u), 0, 0), u=u))
                for u in range(pps)]
    grid_spec = pltpu.PrefetchScalarGridSpec(
        num_scalar_prefetch=1,
        grid=(db, steps),
        in_specs=[tok(), tok(), tok(), pl.BlockSpec((1, 1, width), lambda b, g, pt: (b, 0, 0))]
        + kv_specs + kv_specs + lf_specs,
        out_specs=tok(),
        scratch_shapes=[pltpu.VMEM((pps, width), F32), pltpu.VMEM((pps, width), F32),
                        pltpu.VMEM((nh, 1), F32), pltpu.VMEM((nh, 1), F32),
                        pltpu.VMEM((nh, dh), F32), pltpu.VMEM((1, width), F32)],
    )
    return pl.pallas_call(
        functools.partial(_paged_kernel, pps=pps),
        grid_spec=grid_spec,
        out_shape=jax.ShapeDtypeStruct((db, nh, dh), F32),
        compiler_params=_cparams("parallel", "arbitrary"),
        name="paged_fox",
    )(page_table.reshape(-1), q, k_new, v_new, lf_new_flat, *([cache_k] * pps), *([cache_v] * pps),
      *([cache_lf_flat] * pps))


def _mstep_kernel(q_ref, k_ref, v_ref, ig_ref, lf_ref, c_ref, n_ref, m_ref,
                  h_ref, c_out_ref, n_out_ref, m_out_ref):
    for bb in range(q_ref.shape[0]):
        for hh in range(N_MLSTM_HEADS):
            q = q_ref[bb, hh].astype(F32)
            k = k_ref[bb, hh].astype(F32)
            v = v_ref[bb, hh].astype(F32)
            ig = ig_ref[bb, hh]
            lf = lf_ref[bb, hh]
            c0 = c_ref[bb, hh]
            n0 = n_ref[bb, hh]
            m0 = m_ref[bb, hh]
            inter = lf + m0
            m = jnp.maximum(inter, ig)
            w_intra = jnp.exp(ig - m)
            w_inter = jnp.exp(inter - m)
            s = jnp.sum(q * k, axis=0, keepdims=True) * w_intra
            num = s * v + w_inter * jnp.sum(q * c0, axis=0, keepdims=True)
            den = s + w_inter * jnp.sum(q * n0, axis=0, keepdims=True)
            h_ref[bb, hh] = num / jnp.maximum(jnp.abs(den), jnp.exp(-m))
            c_out_ref[bb, hh] = w_inter * c0 + w_intra * (k * v)
            n_out_ref[bb, hh] = w_inter * n0 + w_intra * k
            m_out_ref[bb, hh] = m


MSTEP_ROWS = 4


def _mlstm_step(q_col, k_col, v_row, ig, lf, state_c, state_n_col, state_m):
    db, hm, dk, dv = state_c.shape
    tb = MSTEP_ROWS if db % MSTEP_ROWS == 0 else 1
    spec = lambda a, b_: pl.BlockSpec((tb, hm, a, b_), lambda i: (i, 0, 0, 0))
    return pl.pallas_call(
        _mstep_kernel,
        grid=(db // tb,),
        in_specs=[spec(dk, 1), spec(dk, 1), spec(1, dv), spec(1, 1), spec(1, 1), spec(dk, dv), spec(dk, 1), spec(1, 1)],
        out_specs=(spec(1, dv), spec(dk, dv), spec(dk, 1), spec(1, 1)),
        out_shape=(jax.ShapeDtypeStruct((db, hm, 1, dv), F32), jax.ShapeDtypeStruct((db, hm, dk, dv), F32),
                   jax.ShapeDtypeStruct((db, hm, dk, 1), F32), jax.ShapeDtypeStruct((db, hm, 1, 1), F32)),
        compiler_params=_cparams("parallel"),
        name="mlstm_step",
    )(q_col, k_col, v_row, ig, lf, state_c, state_n_col, state_m)


def _mix_kernel(att_ref, hm_ref, og_ref, h_ref, gm_ref, w_ref, lg_ref, lb_ref, wrh_ref, wrl_ref, br_ref,
                h1_ref, h1p_ref, e_ref, gate_ref, *, alpha):
    d_att = att_ref.shape[-1]
    parts = []
    for hh in range(N_MLSTM_HEADS):
        vv = slice(hh * MLSTM_V_DIM, (hh + 1) * MLSTM_V_DIM)
        x = hm_ref[:, vv]
        xn = x * lax.rsqrt(jnp.mean(x * x, axis=-1, keepdims=True) + RMS_EPS) * gm_ref[:, vv]
        parts.append((og_ref[:, vv] * xn).astype(BF16))
    mix = _dot(att_ref[...].astype(BF16), w_ref[0:d_att, :])
    for hh in range(N_MLSTM_HEADS):
        r0 = d_att + hh * MLSTM_V_DIM
        mix = mix + _dot(parts[hh], w_ref[r0:r0 + MLSTM_V_DIM, :])
    h1 = _layer_norm(alpha * h_ref[...] + mix, lg_ref[...], lb_ref[...])
    h1_ref[...] = h1
    half = h1.shape[-1] // 2
    hi = lax.bitcast_convert_type(h1[:, :half].astype(BF16).astype(F32), jnp.uint32)
    lo = lax.bitcast_convert_type(h1[:, half:].astype(BF16).astype(F32), jnp.uint32)
    h1p_ref[...] = hi | (lo >> 16)

    logits = _dot3(h1, wrh_ref[...], wrl_ref[...]) + br_ref[...]
    lane = lax.broadcasted_iota(I32, logits.shape, 1).astype(F32)
    e_out = jnp.zeros(logits.shape, F32)
    top = []
    for kk in range(TOP_K):
        mx = jnp.max(logits, axis=-1, keepdims=True)
        idx = jnp.min(jnp.where(logits == mx, lane, float(LANES)), axis=-1, keepdims=True)
        logits = jnp.where(lane == idx, -jnp.inf, logits)
        e_out = jnp.where(lane == float(kk), idx, e_out)
        top.append(mx)
    ex = [jnp.exp(t - top[0]) for t in top]
    inv = 1.0 / (ex[0] + ex[1] + ex[2] + ex[3])
    gate = jnp.zeros(logits.shape, F32)
    for kk in range(TOP_K):
        gate = jnp.where(lane == float(kk), ex[kk] * inv, gate)
    e_ref[...] = e_out.astype(I32)
    gate_ref[...] = gate


def _mix(att, hm, og, h, prm, tm, alpha, packed_rows=None):
    rows, d = h.shape
    packed_rows = rows if packed_rows is None else packed_rows
    d_att = att.shape[-1]
    d_mv = hm.shape[-1]
    row = lambda w: pl.BlockSpec((tm, w), lambda i: (i, 0))
    const = lambda r, c: pl.BlockSpec((r, c), lambda i: (0, 0), pipeline_mode=pl.Buffered(1))
    return pl.pallas_call(
        functools.partial(_mix_kernel, alpha=alpha),
        grid=(rows // tm,),
        in_specs=[row(d_att), row(d_mv), row(d_mv), row(d), const(1, d_mv), const(d_att + d_mv, d),
                  const(1, d), const(1, d), const(d, LANES), const(d, LANES), const(1, LANES)],
        out_specs=(row(d), row(d // 2), row(LANES), row(LANES)),
        out_shape=(jax.ShapeDtypeStruct((rows, d), F32), jax.ShapeDtypeStruct((packed_rows, d // 2), jnp.uint32),
                   jax.ShapeDtypeStruct((rows, LANES), I32), jax.ShapeDtypeStruct((rows, LANES), F32)),
        compiler_params=_cparams("parallel"),
        name="mix",
    )(att, hm, og, h, prm["g_m_out"], prm["w_out"], prm["ln1_g"], prm["ln1_b"], prm["wr_hi"], prm["wr_lo"],
      prm["b_router"])


def _moe_kernel(ie_ref, ns_ref, iv_ref, st_ref, sp_ref,
                x_hbm, wg_ref, wu_ref, bg_ref, bu_ref, wd_ref, bd_ref,
                y_hbm, xbuf, xb, acc, gsem, ssem, *, n_tok, n_items, n_ft):
    i = pl.program_id(0)
    f = pl.program_id(1)
    nf = pl.num_programs(1)
    slot = i % 2
    other = 1 - slot
    n_sub = ns_ref[i]
    valid = iv_ref[i] == 1
    half = xbuf.shape[-1]
    dump_base = TOP_K * n_tok

    def gather_rows(item, sl, row0, count):
        base = st_ref[item] + row0
        dst = xbuf.at[sl, pl.ds(row0, count), :]
        for r in range(count):
            pair = sp_ref[base + r]
            tok = jnp.maximum(pair, 0) >> LOG2_TOP_K
            pltpu.make_async_copy(x_hbm.at[pl.ds(tok, 1), :], dst.at[pl.ds(r, 1), :], gsem.at[sl]).start()

    def scatter_rows(item, sl, row0, count):
        base = st_ref[item] + row0
        n_real = ns_ref[item] * MOE_SUB - row0
        dump = dump_base + sl * MOE_ROWS + row0
        src = acc.at[sl, pl.ds(row0, count), :]
        for r in range(count):
            pair = sp_ref[base + r]
            real = jnp.logical_and(pair >= 0, r < n_real)
            dst = jnp.where(real, (pair & (TOP_K - 1)) * n_tok + (pair >> LOG2_TOP_K), dump + r)
            pltpu.make_async_copy(src.at[pl.ds(r, 1), :], y_hbm.at[pl.ds(dst, 1), :], ssem.at[sl]).start()

    def gather_wait(sl):
        pltpu.make_async_copy(x_hbm.at[pl.ds(0, MOE_ROWS), :], xbuf.at[sl], gsem.at[sl]).wait()

    def scatter_wait(sl):
        pltpu.make_async_copy(acc.at[sl], y_hbm.at[pl.ds(0, MOE_ROWS), :], ssem.at[sl]).wait()

    def all_steps(fn):
        def body(step, carry):
            fn(pl.multiple_of(step * MOE_SUB, MOE_SUB))
            return carry
        lax.fori_loop(0, MOE_ROWS // MOE_SUB, body, 0)

    nxt = jnp.minimum(i + 1, n_items - 1)
    has_next = jnp.logical_and(i + 1 < n_items, iv_ref[nxt] == 1)
    prv = jnp.maximum(i - 1, 0)
    has_prev = jnp.logical_and(i >= 1, iv_ref[prv] == 1)
    prv2 = jnp.maximum(i - 2, 0)
    has_prev2 = jnp.logical_and(i >= 2, iv_ref[prv2] == 1)

    @pl.when(f == 0)
    def _():
        @pl.when(jnp.logical_and(i == 0, valid))
        def _():
            all_steps(lambda row0: gather_rows(0, 0, row0, MOE_SUB))

        @pl.when(valid)
        def _():
            gather_wait(slot)
            packed = xbuf[slot]
            hi = lax.bitcast_convert_type(packed & jnp.uint32(0xFFFF0000), F32)
            lo = lax.bitcast_convert_type(packed << 16, F32)
            xb[:, 0:half] = hi.astype(BF16)
            xb[:, half:2 * half] = lo.astype(BF16)

        @pl.when(has_prev2)
        def _():
            scatter_wait(slot)

        @pl.when(valid)
        def _():
            acc[slot] = jnp.broadcast_to(bd_ref[0], acc.shape[1:])

    per_step = MOE_ROWS // n_ft
    row0 = pl.multiple_of(f * per_step, per_step)
    per_block = MOE_BLOCK // MOE_SUB
    n_full = n_sub // per_block
    max_full = MOE_ROWS // MOE_BLOCK
    share = per_step // max_full
    fused = jnp.logical_and(jnp.logical_and(valid, n_full == max_full), jnp.logical_and(has_next, has_prev))

    @pl.when(jnp.logical_and(has_next, jnp.logical_not(fused)))
    def _():
        gather_rows(i + 1, other, row0, per_step)

    @pl.when(jnp.logical_and(has_prev, jnp.logical_not(fused)))
    def _():
        scatter_rows(i - 1, other, row0, per_step)

    @pl.when(valid)
    def _():
        wg = wg_ref[0].astype(BF16)
        wu = wu_ref[0].astype(BF16)
        wd = wd_ref[0].astype(BF16)

        def rows_block(start, count):
            rows = pl.ds(start, count)
            xs = xb[rows, :]
            gp = _dot(xs, wg) + bg_ref[0]
            up = _dot(xs, wu) + bu_ref[0]
            gp = jnp.minimum(gp, SWIGLU_LIMIT)
            up = jnp.clip(up, -SWIGLU_LIMIT, SWIGLU_LIMIT)
            act = gp * _sigmoid(SWIGLU_ALPHA * gp) * (up + 1.0)
            acc[slot, rows, :] += _dot(act.astype(BF16), wd)

        def body(blk, carry):
            rows_block(pl.multiple_of(blk * MOE_BLOCK, MOE_BLOCK), MOE_BLOCK)
            return carry

        def body_fused(blk, carry):
            r0 = pl.multiple_of(row0 + blk * share, 8)
            gather_rows(i + 1, other, r0, share)
            scatter_rows(i - 1, other, r0, share)
            rows_block(pl.multiple_of(blk * MOE_BLOCK, MOE_BLOCK), MOE_BLOCK)
            return carry

        @pl.when(fused)
        def _():
            lax.fori_loop(0, max_full, body_fused, 0)

        @pl.when(jnp.logical_not(fused))
        def _():
            lax.fori_loop(0, n_full, body, 0)

        tail = n_sub - n_full * per_block
        for nn in range(1, per_block):
            @pl.when(tail == nn)
            def _():
                rows_block(pl.multiple_of(n_full * MOE_BLOCK, MOE_BLOCK), nn * MOE_SUB)

    @pl.when(jnp.logical_and(i == n_items - 1, f == nf - 1))
    def _():
        @pl.when(valid)
        def _():
            all_steps(lambda row0: scatter_rows(i, slot, row0, MOE_SUB))
            scatter_wait(slot)

        @pl.when(has_prev)
        def _():
            scatter_wait(other)


def _moe(x_packed, item_e, item_ns, item_valid, item_start, slot_pair, w_gate_up, b_gate_up, w_down, b_down):
    n_tok, half = x_packed.shape
    d = 2 * half
    n_exp, _, two_f = w_gate_up.shape
    nf = two_f // 2 // MOE_FT
    n_items = item_e.shape[0]
    assert MOE_ROWS % nf == 0 and (MOE_ROWS // nf) % (8 * (MOE_ROWS // MOE_BLOCK)) == 0

    def ftile(i, f, iv):
        return f * iv[i] + (nf - 1) * (1 - iv[i])

    grid_spec = pltpu.PrefetchScalarGridSpec(
        num_scalar_prefetch=5,
        grid=(n_items, nf),
        in_specs=[
            pl.BlockSpec(memory_space=pl.ANY),
            pl.BlockSpec((1, d, MOE_FT), lambda i, f, ie, ns, iv, st, sp: (ie[i], 0, ftile(i, f, iv))),
            pl.BlockSpec((1, d, MOE_FT), lambda i, f, ie, ns, iv, st, sp: (ie[i], 0, nf + ftile(i, f, iv))),
            pl.BlockSpec((1, 1, MOE_FT), lambda i, f, ie, ns, iv, st, sp: (ie[i], 0, ftile(i, f, iv))),
            pl.BlockSpec((1, 1, MOE_FT), lambda i, f, ie, ns, iv, st, sp: (ie[i], 0, nf + ftile(i, f, iv))),
            pl.BlockSpec((1, MOE_FT, d), lambda i, f, ie, ns, iv, st, sp: (ie[i], ftile(i, f, iv), 0)),
            pl.BlockSpec((1, 1, d), lambda i, f, ie, ns, iv, st, sp: (ie[i], 0, 0)),
        ],
        out_specs=pl.BlockSpec(memory_space=pl.ANY),
        scratch_shapes=[pltpu.VMEM((2, MOE_ROWS, half), jnp.uint32), pltpu.VMEM((MOE_ROWS, d), BF16),
                        pltpu.VMEM((2, MOE_ROWS, d), F32),
                        pltpu.SemaphoreType.DMA((2,)), pltpu.SemaphoreType.DMA((2,))],
    )
    return pl.pallas_call(
        functools.partial(_moe_kernel, n_tok=n_tok, n_items=n_items, n_ft=nf),
        grid_spec=grid_spec,
        out_shape=jax.ShapeDtypeStruct((TOP_K * n_tok + 2 * MOE_ROWS, d), F32),
        compiler_params=_cparams("arbitrary", "arbitrary"),
        name="moe",
    )(item_e, item_ns, item_valid, item_start, slot_pair, x_packed, w_gate_up, w_gate_up,
      b_gate_up.reshape(n_exp, 1, two_f), b_gate_up.reshape(n_exp, 1, two_f), w_down, b_down.reshape(n_exp, 1, d))


def _combine_kernel(y0_ref, y1_ref, y2_ref, y3_ref, gate_ref, h1_ref, lg_ref, lb_ref, o_ref, *, alpha):
    acc = alpha * h1_ref[...]
    for kk, y_ref in enumerate((y0_ref, y1_ref, y2_ref, y3_ref)):
        acc = acc + y_ref[...] * gate_ref[:, kk:kk + 1]
    o_ref[0] = _layer_norm(acc, lg_ref[...], lb_ref[...])


def _combine(y_pairs, gate, h1, ln_g, ln_b, alpha, *, n_tok, tok0, group_rows, skip, out_rows, tm):
    d = h1.shape[-1]
    groups = h1.shape[0] // group_rows
    assert out_rows % tm == 0 and skip % 8 == 0 and group_rows % 8 == 0 and tok0 % 8 == 0 and n_tok % 8 == 0

    def rows(base):
        return lambda b, j: (pl.multiple_of(base + b * group_rows + skip + j * tm, 8), 0)

    def el(w, base):
        return pl.BlockSpec((pl.Element(tm), pl.Element(w)), rows(base))

    const = lambda c: pl.BlockSpec((1, c), lambda b, j: (0, 0))
    y_specs = [el(d, kk * n_tok + tok0) for kk in range(TOP_K)]
    return pl.pallas_call(
        functools.partial(_combine_kernel, alpha=alpha),
        grid=(groups, out_rows // tm),
        in_specs=y_specs + [el(LANES, 0), el(d, 0), const(d), const(d)],
        out_specs=pl.BlockSpec((1, tm, d), lambda b, j: (b, j, 0)),
        out_shape=jax.ShapeDtypeStruct((groups, out_rows, d), F32),
        compiler_params=_cparams("parallel", "parallel"),
        name="combine",
    )(*([y_pairs] * TOP_K), gate, h1, ln_g, ln_b)


def _route(top_e):
    n_pairs = top_e.shape[0] * TOP_K
    per_item = MOE_ROWS // MOE_SUB
    pair_e = top_e.reshape(-1)
    order = jnp.argsort(pair_e, stable=True).astype(I32)
    e_sorted = pair_e[order]
    edges = jnp.arange(N_EXPERTS + 1, dtype=I32)
    bounds = jnp.sum((e_sorted[None, :] < edges[:, None]).astype(I32), axis=1)
    start = bounds[:-1]
    counts = bounds[1:] - start
    n_sb = (counts + MOE_SUB - 1) // MOE_SUB
    sb_end = jnp.cumsum(n_sb)
    sb_start = sb_end - n_sb
    max_sb = n_pairs // MOE_SUB + N_EXPERTS
    slot = jnp.arange(max_sb * MOE_SUB, dtype=I32)
    e_slot = jnp.minimum(jnp.sum((slot[:, None] >= (sb_end * MOE_SUB)[None, :]).astype(I32), axis=1), N_EXPERTS - 1)
    rank = slot - sb_start[e_slot] * MOE_SUB
    src = jnp.clip(start[e_slot] + rank, 0, n_pairs - 1)
    slot_pair = jnp.where(rank < counts[e_slot], order[src], -1).astype(I32)
    slot_pair = jnp.concatenate([slot_pair, jnp.full((MOE_ROWS,), -1, I32)])

    n_it = (n_sb + per_item - 1) // per_item
    it_end = jnp.cumsum(n_it)
    it_start = it_end - n_it
    max_items = (max_sb + (per_item - 1) * N_EXPERTS) // per_item
    idx = jnp.arange(max_items, dtype=I32)
    n_valid = it_end[-1]
    item_valid = (idx < n_valid).astype(I32)
    idc = jnp.minimum(idx, n_valid - 1)
    item_e = jnp.minimum(jnp.sum((idc[:, None] >= it_end[None, :]).astype(I32), axis=1), N_EXPERTS - 1).astype(I32)
    chunk = idc - it_start[item_e]
    item_start = ((sb_start[item_e] + per_item * chunk) * MOE_SUB).astype(I32)
    item_ns = (jnp.clip(n_sb[item_e] - per_item * chunk, 0, per_item) * item_valid).astype(I32)
    return item_e, item_ns, item_valid, item_start, slot_pair


def _prep_params(ln_in_g, ln_in_b, w_in, b_att_f, g_att_q, g_att_k, b_m_i, b_m_f, g_m_out, w_out,
                 ln1_g, ln1_b, w_router, b_router, ln2_g, ln2_b):
    d = w_in.shape[0]
    d_att = N_ATT_HEADS * ATT_HEAD_DIM
    d_mqk = N_MLSTM_HEADS * MLSTM_QK_DIM
    d_mv = N_MLSTM_HEADS * MLSTM_V_DIM
    sizes = (d_att, d_att, d_att, N_ATT_HEADS, d_mqk, d_mqk, d_mv, N_MLSTM_HEADS, N_MLSTM_HEADS, d_mv)
    offs = [0]
    for s in sizes:
        offs.append(offs[-1] + s)
    cols = lambda i: w_in[:, offs[i]:offs[i + 1]]
    w_main = jnp.concatenate([cols(0), cols(1), cols(2), cols(4), cols(5), cols(6), cols(9)], axis=1).astype(BF16)
    w_gate = jnp.concatenate([cols(3), cols(7), cols(8), jnp.zeros((d, LANES - N_GATES), F32)], axis=1)
    wg_hi, wg_lo = _split2(w_gate)
    gate_bias = jnp.concatenate([b_att_f, b_m_i, b_m_f, jnp.zeros((LANES - N_GATES,), F32)]).reshape(1, LANES)
    w_r = jnp.concatenate([w_router, jnp.zeros((d, LANES - N_EXPERTS), F32)], axis=1)
    wr_hi, wr_lo = _split2(w_r)
    b_r = jnp.concatenate([b_router, jnp.full((LANES - N_EXPERTS,), NEG, F32)]).reshape(1, LANES)
    return dict(
        ln_in_g=ln_in_g.reshape(1, d), ln_in_b=ln_in_b.reshape(1, d), w_main=w_main, wg_hi=wg_hi, wg_lo=wg_lo,
        gate_bias=gate_bias, g_att_q=g_att_q.reshape(1, -1), g_att_k=g_att_k.reshape(1, -1),
        g_m_out=g_m_out.reshape(1, -1), w_out=w_out.astype(BF16), ln1_g=ln1_g.reshape(1, d),
        ln1_b=ln1_b.reshape(1, d), wr_hi=wr_hi, wr_lo=wr_lo, b_router=b_r,
        ln2_g=ln2_g.reshape(1, d), ln2_b=ln2_b.reshape(1, d))


def kernel(x_prompt, x_sample, cache_k, cache_v, cache_logf, state_C, state_n, state_m, page_table,
           meta_tokens, ln_in_g, ln_in_b, w_in, b_att_f, g_att_q, g_att_k, b_m_i, b_m_f, g_m_out, w_out,
           ln1_g, ln1_b, w_router, b_router, w_gate_up, b_gate_up, w_down, b_down, ln2_g, ln2_b):
    depth = w_in.shape[0]
    assert depth == 1, "single-layer trunk only"
    nb, seq, d = x_prompt.shape
    db = x_sample.shape[0]
    assert x_sample.shape[1] == 1, "one new token per sample"
    alpha = (2.0 * depth) ** 0.25
    d_att = N_ATT_HEADS * ATT_HEAD_DIM
    hm = N_MLSTM_HEADS
    prm = _prep_params(ln_in_g, ln_in_b, w_in[0], b_att_f[0], g_att_q[0], g_att_k[0], b_m_i[0], b_m_f[0],
                       g_m_out[0], w_out[0], ln1_g[0], ln1_b[0], w_router[0], b_router[0], ln2_g[0], ln2_b[0])

    length = N_META + seq
    lp = pl.cdiv(length, CHUNK) * CHUNK
    meta = jnp.broadcast_to(meta_tokens[None], (nb, N_META, d))
    xp = jnp.pad(x_prompt, ((0, 0), (N_META, lp - length), (0, 0)))
    xp = lax.dynamic_update_slice(xp, meta, (0, 0, 0))
    (hp, qa, k32, kb, v32, vb, gates, qm, km, vm, og) = _proj(xp, length, lp // 8, prm)
    cum_l, cum_lt, gates_t, k_aug, q_aug = _gate_scan(gates)
    att_p = _fox_prompt(qa, q_aug, kb, k_aug, vb)
    hm_p, c_p, n_p, m_p = _mlstm_prompt(qm, km, vm, gates, cum_l, gates_t, cum_lt)
    rows_p = nb * lp
    h1_p, h1p_p, e_p, gate_p = _mix(att_p.reshape(rows_p, d_att), hm_p.reshape(rows_p, -1), og.reshape(rows_p, -1),
                                    hp.reshape(rows_p, d), prm, 256, alpha, packed_rows=rows_p + db)

    xs = x_sample.reshape(1, db, d)
    (hs, qa_s, k32_s, _, v32_s, _, gates_s, qm_s, km_s, vm_s, og_s) = _proj(xs, db, db, prm)
    n_pool, page = cache_k.shape[1], cache_k.shape[2]
    nh, dh = N_ATT_HEADS, ATT_HEAD_DIM
    lf_s = gates_s[0, :, GATE_LF_ATT:GATE_LF_ATT + nh]
    att_s = _paged_attention(
        qa_s.astype(F32).reshape(db, nh, dh), k32_s.reshape(db, nh, dh), v32_s.reshape(db, nh, dh),
        jnp.tile(lf_s, (1, page)).reshape(db, 1, page * nh), cache_k, cache_v,
        cache_logf[0].reshape(n_pool, 1, page * nh), page_table)
    ig_s = gates_s[0, :, GATE_IG:GATE_IG + hm].reshape(db, hm, 1, 1)
    lfm_s = gates_s[0, :, GATE_LF_M:GATE_LF_M + hm].reshape(db, hm, 1, 1)
    hm_s, c_s, n_s, m_s = _mlstm_step(
        qm_s.astype(F32).reshape(db, hm, MLSTM_QK_DIM, 1), km_s.astype(F32).reshape(db, hm, MLSTM_QK_DIM, 1),
        vm_s.astype(F32).reshape(db, hm, 1, MLSTM_V_DIM), ig_s, lfm_s, state_C[0],
        state_n[0].reshape(db, hm, MLSTM_QK_DIM, 1), state_m[0].reshape(db, hm, 1, 1))
    h1_s, h1p_s, e_s, gate_s = _mix(att_s.reshape(db, d_att), hm_s.reshape(db, -1), og_s.reshape(db, -1),
                                    hs.reshape(db, d), prm, db, alpha)

    n_tok = rows_p + db
    x_packed = lax.dynamic_update_slice(h1p_p, h1p_s, (rows_p, 0))
    top_e = jnp.concatenate([e_p, e_s], axis=0)[:, :TOP_K]
    item_e, item_ns, item_valid, item_start, slot_pair = _route(top_e)
    y_pairs = _moe(x_packed, item_e, item_ns, item_valid, item_start, slot_pair,
                   w_gate_up[0], b_gate_up[0], w_down[0], b_down[0])
    y_prompt = _combine(y_pairs, gate_p, h1_p, prm["ln2_g"], prm["ln2_b"], alpha, n_tok=n_tok, tok0=0,
                        group_rows=lp, skip=N_META, out_rows=seq, tm=256)
    y_sample = _combine(y_pairs, gate_s, h1_s, prm["ln2_g"], prm["ln2_b"], alpha, n_tok=n_tok, tok0=rows_p,
                        group_rows=db, skip=0, out_rows=db, tm=db).reshape(db, 1, d)
    k_prompt = k32.reshape(1, nb, length, N_ATT_HEADS, ATT_HEAD_DIM)
    v_prompt = v32.reshape(1, nb, length, N_ATT_HEADS, ATT_HEAD_DIM)
    lf_prompt = gates[:, :length, GATE_LF_ATT:GATE_LF_ATT + N_ATT_HEADS][None]
    return (y_prompt, y_sample, k_prompt, v_prompt, lf_prompt,
            c_p[None], n_p.reshape(1, nb, hm, MLSTM_QK_DIM), m_p[:, :, 0, 0][None],
            k32_s.reshape(1, db, 1, N_ATT_HEADS, ATT_HEAD_DIM), v32_s.reshape(1, db, 1, N_ATT_HEADS, ATT_HEAD_DIM),
            lf_s.reshape(1, db, 1, nh),
            c_s[None], n_s.reshape(1, db, hm, MLSTM_QK_DIM), m_s.reshape(1, db, hm))
```

```python
import functools

import jax
import jax.numpy as jnp
from jax import lax
from jax.experimental import pallas as pl
from jax.experimental.pallas import tpu as pltpu

F32 = jnp.float32
BF16 = jnp.bfloat16
I32 = jnp.int32

N_META = 16
CHUNK = 128
ATT_HEAD_DIM = 128
N_ATT_HEADS = 8
N_MLSTM_HEADS = 4
MLSTM_QK_DIM = 128
MLSTM_V_DIM = 256
N_EXPERTS = 32
TOP_K = 4
LOG2_TOP_K = 2
SWIGLU_ALPHA = 1.702
SWIGLU_LIMIT = 7.0
LN_EPS = 1e-5
RMS_EPS = 1e-6
NEG = -1e30

LANES = 128
GATE_LF_ATT = 0
GATE_IG = N_ATT_HEADS
GATE_LF_M = GATE_IG + N_MLSTM_HEADS
N_GATES = GATE_LF_M + N_MLSTM_HEADS

VMEM_LIMIT = 56 * 1024 * 1024

MOE_ROWS = 1152
MOE_SUB = 128
MOE_FT = 256
MOE_BLOCK = 512
PAGES_PER_STEP = 16
FOX_TQ = 256


def _cparams(*sem):
    return pltpu.CompilerParams(dimension_semantics=sem, vmem_limit_bytes=VMEM_LIMIT)


def _split2(x):
    hi = x.astype(BF16)
    lo = (x - hi.astype(F32)).astype(BF16)
    return hi, lo


def _split3(x):
    hi = x.astype(BF16)
    r1 = x - hi.astype(F32)
    mid = r1.astype(BF16)
    lo = (r1 - mid.astype(F32)).astype(BF16)
    return hi, mid, lo


def _dot(a, b):
    return jnp.dot(a, b, preferred_element_type=F32)


def _dot_nt(a, b):
    return lax.dot_general(a, b, (((1,), (1,)), ((), ())), preferred_element_type=F32)


def _dot_tn(a, b):
    return lax.dot_general(a, b, (((0,), (0,)), ((), ())), preferred_element_type=F32)


def _dot3(x_f32, w_hi, w_lo):
    x_hi, x_lo = _split2(x_f32)
    return _dot(x_hi, w_hi) + (_dot(x_lo, w_hi) + _dot(x_hi, w_lo))


def _layer_norm(x, g, b):
    mu = jnp.mean(x, axis=-1, keepdims=True)
    xc = x - mu
    var = jnp.mean(xc * xc, axis=-1, keepdims=True)
    return xc * lax.rsqrt(var + LN_EPS) * g + b


def _log_sigmoid(z):
    return jnp.minimum(z, 0.0) - jnp.log(1.0 + jnp.exp(-jnp.abs(z)))


def _sigmoid(z):
    return 1.0 / (1.0 + jnp.exp(-z))


def _proj_kernel(x_ref, lng_ref, lnb_ref, w_ref, wgh_ref, wgl_ref, gb_ref, gq_ref, gk_ref,
                 h_ref, q_ref, k32_ref, kb_ref, v32_ref, vb_ref, g_ref, qm_ref, km_ref, vm_ref, og_ref,
                 *, tm, valid_len):
    d_att = N_ATT_HEADS * ATT_HEAD_DIM
    d_mqk = N_MLSTM_HEADS * MLSTM_QK_DIM
    d_mv = N_MLSTM_HEADS * MLSTM_V_DIM
    h = _layer_norm(x_ref[0], lng_ref[...], lnb_ref[...])
    h_ref[0] = h
    hb = h.astype(BF16)

    def seg(c0, width):
        return _dot(hb, w_ref[:, c0:c0 + width])

    scale = ATT_HEAD_DIM ** -0.5
    qa = seg(0, d_att)
    ka = seg(d_att, d_att)
    for hh in range(N_ATT_HEADS):
        sl = slice(hh * ATT_HEAD_DIM, (hh + 1) * ATT_HEAD_DIM)
        qh = qa[:, sl]
        qn = qh * lax.rsqrt(jnp.mean(qh * qh, axis=-1, keepdims=True) + RMS_EPS) * gq_ref[...]
        q_ref[0, :, sl] = (qn * scale).astype(BF16)
        kh = ka[:, sl]
        kn = kh * lax.rsqrt(jnp.mean(kh * kh, axis=-1, keepdims=True) + RMS_EPS) * gk_ref[...]
        k32_ref[0, :, sl] = kn
        kb_ref[0, :, sl] = kn.astype(BF16)
    va = seg(2 * d_att, d_att)
    v32_ref[0] = va
    vb_ref[0] = va.astype(BF16)
    c0 = 3 * d_att
    qm_ref[0] = seg(c0, d_mqk).astype(BF16)
    km_ref[0] = (seg(c0 + d_mqk, d_mqk) * (MLSTM_QK_DIM ** -0.5)).astype(BF16)
    vm_ref[0] = seg(c0 + 2 * d_mqk, d_mv).astype(BF16)
    og_ref[0] = _sigmoid(seg(c0 + 2 * d_mqk + d_mv, d_mv))

    z = _dot3(h, wgh_ref[...], wgl_ref[...]) + gb_ref[...]
    lane = lax.broadcasted_iota(I32, z.shape, 1)
    row = pl.program_id(1) * tm + lax.broadcasted_iota(I32, z.shape, 0)
    is_ig = jnp.logical_and(lane >= GATE_IG, lane < GATE_LF_M)
    gates = jnp.where(is_ig, z, jnp.where(lane < N_GATES, _log_sigmoid(z), 0.0))
    gates = jnp.where(row < valid_len, gates, jnp.where(is_ig, NEG, 0.0))
    g_ref[0] = gates


def _proj(x, valid_len, tm, prm):
    nb, lp, d = x.shape
    d_att = N_ATT_HEADS * ATT_HEAD_DIM
    d_mqk = N_MLSTM_HEADS * MLSTM_QK_DIM
    d_mv = N_MLSTM_HEADS * MLSTM_V_DIM
    n_main = prm["w_main"].shape[1]
    grid = (nb, lp // tm)
    row = lambda w: pl.BlockSpec((1, tm, w), lambda b, i: (b, i, 0))
    const = lambda r, c: pl.BlockSpec((r, c), lambda b, i: (0, 0), pipeline_mode=pl.Buffered(1))
    out_shape = (
        jax.ShapeDtypeStruct((nb, lp, d), F32),
        jax.ShapeDtypeStruct((nb, lp, d_att), BF16),
        jax.ShapeDtypeStruct((nb, valid_len, d_att), F32),
        jax.ShapeDtypeStruct((nb, lp, d_att), BF16),
        jax.ShapeDtypeStruct((nb, valid_len, d_att), F32),
        jax.ShapeDtypeStruct((nb, lp, d_att), BF16),
        jax.ShapeDtypeStruct((nb, lp, LANES), F32),
        jax.ShapeDtypeStruct((nb, lp, d_mqk), BF16),
        jax.ShapeDtypeStruct((nb, lp, d_mqk), BF16),
        jax.ShapeDtypeStruct((nb, lp, d_mv), BF16),
        jax.ShapeDtypeStruct((nb, lp, d_mv), F32),
    )
    out_specs = (row(d), row(d_att), row(d_att), row(d_att), row(d_att), row(d_att), row(LANES),
                 row(d_mqk), row(d_mqk), row(d_mv), row(d_mv))
    return pl.pallas_call(
        functools.partial(_proj_kernel, tm=tm, valid_len=valid_len),
        grid=grid,
        in_specs=[row(d), const(1, d), const(1, d), const(d, n_main), const(d, LANES), const(d, LANES),
                  const(1, LANES), const(1, ATT_HEAD_DIM), const(1, ATT_HEAD_DIM)],
        out_specs=out_specs,
        out_shape=out_shape,
        compiler_params=_cparams("parallel", "arbitrary"),
        name="proj",
    )(x, prm["ln_in_g"], prm["ln_in_b"], prm["w_main"], prm["wg_hi"], prm["wg_lo"], prm["gate_bias"],
      prm["g_att_q"], prm["g_att_k"])


def _gate_scan_kernel(g_ref, cl_ref, clt_ref, gt_ref, ka_ref, qa_ref, *, n_chunks):
    r = lax.broadcasted_iota(I32, (CHUNK, CHUNK), 0)
    c = lax.broadcasted_iota(I32, (CHUNK, CHUNK), 1)
    tri = jnp.where(c <= r, 1.0, 0.0).astype(BF16)
    carry = jnp.zeros((1, LANES), F32)
    for ci in range(n_chunks):
        sl = slice(ci * CHUNK, (ci + 1) * CHUNK)
        g = g_ref[0, sl, :]
        hi, mid, lo = _split3(g)
        loc = _dot(tri, hi) + (_dot(tri, mid) + _dot(tri, lo))
        glob = loc + carry
        carry = glob[CHUNK - 1:CHUNK, :]
        cl_ref[0, sl, :] = loc
        clt_ref[0, ci] = loc.T
        gt_ref[0, ci] = g.T
        for hh in range(N_ATT_HEADS):
            cum = glob[:, GATE_LF_ATT + hh:GATE_LF_ATT + hh + 1]
            t0 = cum.astype(BF16).astype(F32)
            r1 = cum - t0
            t1 = r1.astype(BF16).astype(F32)
            t2 = (r1 - t1).astype(BF16).astype(F32)
            hs = slice(hh * ATT_HEAD_DIM, (hh + 1) * ATT_HEAD_DIM)
            ones = jnp.where(c < 3, 1.0, 0.0)
            terms = jnp.where(c == 3, t0, jnp.where(c == 4, t1, jnp.where(c == 5, t2, 0.0)))
            qa_ref[0, sl, hs] = (ones + terms).astype(BF16)
            ones_k = jnp.where(jnp.logical_and(c >= 3, c < 6), 1.0, 0.0)
            terms_k = jnp.where(c == 0, t0, jnp.where(c == 1, t1, jnp.where(c == 2, t2, 0.0)))
            ka_ref[0, sl, hs] = (ones_k - terms_k).astype(BF16)


def _gate_scan(gates):
    nb, lp, _ = gates.shape
    nc = lp // CHUNK
    d_att = N_ATT_HEADS * ATT_HEAD_DIM
    col = pl.BlockSpec((1, lp, LANES), lambda b: (b, 0, 0))
    rowt = pl.BlockSpec((1, nc, LANES, CHUNK), lambda b: (b, 0, 0, 0))
    aug = pl.BlockSpec((1, lp, d_att), lambda b: (b, 0, 0))
    return pl.pallas_call(
        functools.partial(_gate_scan_kernel, n_chunks=nc),
        grid=(nb,),
        in_specs=[col],
        out_specs=(col, rowt, rowt, aug, aug),
        out_shape=(jax.ShapeDtypeStruct((nb, lp, LANES), F32),
                   jax.ShapeDtypeStruct((nb, nc, LANES, CHUNK), F32),
                   jax.ShapeDtypeStruct((nb, nc, LANES, CHUNK), F32),
                   jax.ShapeDtypeStruct((nb, lp, d_att), BF16),
                   jax.ShapeDtypeStruct((nb, lp, d_att), BF16)),
        compiler_params=_cparams("parallel"),
        name="gate_scan",
    )(gates)


def _fox_kernel(q_ref, qa_ref, k_ref, ka_ref, v_ref, o_ref, kx_sc, vt_sc, qx_sc, m_sc, l_sc, acc_sc,
                *, n_chunks, tq):
    ti = pl.program_id(1)
    dh = ATT_HEAD_DIM

    @pl.when(ti == 0)
    def _():
        for j in range(n_chunks):
            rows = slice(j * CHUNK, (j + 1) * CHUNK)
            for hh in range(N_ATT_HEADS):
                sl = slice(hh * dh, (hh + 1) * dh)
                kx_sc[j, :, 2 * hh * dh:(2 * hh + 1) * dh] = k_ref[0, rows, sl]
                kx_sc[j, :, (2 * hh + 1) * dh:(2 * hh + 2) * dh] = ka_ref[0, rows, sl]
                vt_sc[hh, j] = v_ref[0, rows, sl].astype(F32).T.astype(BF16)

    for hh in range(N_ATT_HEADS):
        sl = slice(hh * dh, (hh + 1) * dh)
        qx_sc[:, 2 * hh * dh:(2 * hh + 1) * dh] = q_ref[0, :, sl]
        qx_sc[:, (2 * hh + 1) * dh:(2 * hh + 2) * dh] = qa_ref[0, :, sl]
    m_sc[...] = jnp.full(m_sc.shape, NEG, F32)
    l_sc[...] = jnp.zeros(l_sc.shape, F32)
    acc_sc[...] = jnp.zeros(acc_sc.shape, F32)
    key = lax.broadcasted_iota(I32, (CHUNK, tq), 0)
    qry = lax.broadcasted_iota(I32, (CHUNK, tq), 1)

    def step(j, masked):
        scores = []
        for hh in range(N_ATT_HEADS):
            xs = slice(2 * hh * dh, (2 * hh + 2) * dh)
            scores.append(_dot_nt(kx_sc[j, :, xs], qx_sc[:, xs]))
        probs, alphas = [], []
        for hh in range(N_ATT_HEADS):
            s = scores[hh]
            if masked:
                s = jnp.where(j * CHUNK + key <= ti * tq + qry, s, NEG)
            m_old = m_sc[hh]
            m_new = jnp.maximum(m_old, jnp.max(s, axis=0, keepdims=True))
            p = jnp.exp(s - m_new)
            alpha = jnp.exp(m_old - m_new)
            l_sc[hh] = alpha * l_sc[hh] + jnp.sum(p, axis=0, keepdims=True)
            m_sc[hh] = m_new
            probs.append(p.astype(BF16))
            alphas.append(alpha)
        for hh in range(N_ATT_HEADS):
            acc_sc[hh] = alphas[hh] * acc_sc[hh] + _dot(vt_sc[hh, j], probs[hh])

    def body(j, carry):
        step(j, False)
        return carry

    per = tq // CHUNK
    lax.fori_loop(0, ti * per, body, 0)
    for dj in range(per):
        j = ti * per + dj

        @pl.when(j < n_chunks)
        def _():
            step(j, True)

    for hh in range(N_ATT_HEADS):
        sl = slice(hh * dh, (hh + 1) * dh)
        o_ref[0, :, sl] = (acc_sc[hh] / l_sc[hh]).T.astype(o_ref.dtype)


def _fox_prompt(q, q_aug, kb, k_aug, vb):
    nb, lp, d_att = q.shape
    nc = lp // CHUNK
    tq = FOX_TQ
    blk = pl.BlockSpec((1, tq, d_att), lambda b, i: (b, i, 0))
    full = pl.BlockSpec((1, lp, d_att), lambda b, i: (b, 0, 0))
    nh, dh = N_ATT_HEADS, ATT_HEAD_DIM
    return pl.pallas_call(
        functools.partial(_fox_kernel, n_chunks=nc, tq=tq),
        grid=(nb, pl.cdiv(lp, tq)),
        in_specs=[blk, blk, full, full, full],
        out_specs=blk,
        out_shape=jax.ShapeDtypeStruct((nb, lp, d_att), BF16),
        scratch_shapes=[pltpu.VMEM((nc, CHUNK, 2 * d_att), BF16),
                        pltpu.VMEM((nh, nc, dh, CHUNK), BF16),
                        pltpu.VMEM((tq, 2 * d_att), BF16),
                        pltpu.VMEM((nh, 1, tq), F32), pltpu.VMEM((nh, 1, tq), F32),
                        pltpu.VMEM((nh, dh, tq), F32)],
        compiler_params=_cparams("parallel", "arbitrary"),
        name="fox_prompt",
    )(q, q_aug, kb, k_aug, vb)


def _mlstm_kernel(q_ref, k_ref, v_ref, g_ref, cl_ref, gt_ref, clt_ref,
                  h_ref, c_out_ref, n_out_ref, m_out_ref, c_sc, n_sc, m_sc):
    ci = pl.program_id(1)

    @pl.when(ci == 0)
    def _():
        c_sc[...] = jnp.zeros(c_sc.shape, F32)
        n_sc[...] = jnp.zeros(n_sc.shape, F32)
        m_sc[...] = jnp.zeros(m_sc.shape, F32)

    r = lax.broadcasted_iota(I32, (CHUNK, CHUNK), 0)
    c = lax.broadcasted_iota(I32, (CHUNK, CHUNK), 1)
    causal = c <= r
    last = slice(CHUNK - 1, CHUNK)
    heads = range(N_MLSTM_HEADS)
    qk = [slice(hh * MLSTM_QK_DIM, (hh + 1) * MLSTM_QK_DIM) for hh in heads]
    vv = [slice(hh * MLSTM_V_DIM, (hh + 1) * MLSTM_V_DIM) for hh in heads]
    qh = [q_ref[0, :, qk[hh]] for hh in heads]
    kh = [k_ref[0, :, qk[hh]] for hh in heads]
    vh = [v_ref[0, :, vv[hh]] for hh in heads]
    c0 = [c_sc[hh] for hh in heads]
    qk_raw = [_dot_nt(qh[hh], kh[hh]) for hh in heads]
    q_c0 = [_dot(qh[hh], c0[hh].astype(BF16)) for hh in heads]

    s_bf, kw_bf, stash = [], [], []
    for hh in heads:
        ig_l, lf_l = GATE_IG + hh, GATE_LF_M + hh
        b_col = cl_ref[0, :, lf_l:lf_l + 1]
        ig_col = g_ref[0, :, ig_l:ig_l + 1]
        b_row = clt_ref[0, 0, lf_l:lf_l + 1, :]
        ig_row = gt_ref[0, 0, ig_l:ig_l + 1, :]
        m0 = m_sc[hh][:, 0:1]
        n0 = n_sc[hh]
        dmat = jnp.where(causal, b_col - b_row + ig_row, NEG)
        inter = b_col + m0
        mt = jnp.maximum(inter, jnp.max(dmat, axis=-1, keepdims=True))
        w_inter = jnp.exp(inter - mt)
        s = qk_raw[hh] * jnp.exp(dmat - mt)
        den = jnp.sum(s, axis=-1, keepdims=True) + w_inter * jnp.sum(qh[hh].astype(F32) * n0, axis=-1, keepdims=True)
        m_new = mt[last, :]
        b_last = b_col[last, :]
        w_state = jnp.exp(b_last - b_col + ig_col - m_new)
        decay = jnp.exp(b_last + m0 - m_new)
        kw = kh[hh].astype(F32) * w_state
        s_bf.append(s.astype(BF16))
        kw_bf.append(kw.astype(BF16))
        stash.append((w_inter, den, mt, m_new, decay, decay * n0 + jnp.sum(kw, axis=0, keepdims=True)))

    s_v = [_dot(s_bf[hh], vh[hh]) for hh in heads]
    kw_v = [_dot_tn(kw_bf[hh], vh[hh]) for hh in heads]
    for hh in heads:
        w_inter, den, mt, m_new, decay, n_new = stash[hh]
        num = s_v[hh] + w_inter * q_c0[hh]
        h_ref[0, :, vv[hh]] = num / jnp.maximum(jnp.abs(den), jnp.exp(-mt))
        c_new = decay * c0[hh] + kw_v[hh]
        c_sc[hh] = c_new
        n_sc[hh] = n_new
        m_sc[hh] = jnp.broadcast_to(m_new, (1, LANES))
        c_out_ref[0, hh] = c_new
        n_out_ref[0, hh] = n_new
        m_out_ref[0, hh] = jnp.broadcast_to(m_new, (1, LANES))


def _mlstm_prompt(qm, km, vm, gates, cum_l, gates_t, cum_lt):
    nb, lp, d_mqk = qm.shape
    d_mv = vm.shape[-1]
    nc = lp // CHUNK
    hm = N_MLSTM_HEADS
    blk = lambda w: pl.BlockSpec((1, CHUNK, w), lambda b, i: (b, i, 0))
    tblk = pl.BlockSpec((1, 1, LANES, CHUNK), lambda b, i: (b, i, 0, 0))
    return pl.pallas_call(
        _mlstm_kernel,
        grid=(nb, nc),
        in_specs=[blk(d_mqk), blk(d_mqk), blk(d_mv), blk(LANES), blk(LANES), tblk, tblk],
        out_specs=(blk(d_mv),
                   pl.BlockSpec((1, hm, MLSTM_QK_DIM, MLSTM_V_DIM), lambda b, i: (b, 0, 0, 0)),
                   pl.BlockSpec((1, hm, 1, MLSTM_QK_DIM), lambda b, i: (b, 0, 0, 0)),
                   pl.BlockSpec((1, hm, 1, LANES), lambda b, i: (b, 0, 0, 0))),
        out_shape=(jax.ShapeDtypeStruct((nb, lp, d_mv), F32),
                   jax.ShapeDtypeStruct((nb, hm, MLSTM_QK_DIM, MLSTM_V_DIM), F32),
                   jax.ShapeDtypeStruct((nb, hm, 1, MLSTM_QK_DIM), F32),
                   jax.ShapeDtypeStruct((nb, hm, 1, LANES), F32)),
        scratch_shapes=[pltpu.VMEM((hm, MLSTM_QK_DIM, MLSTM_V_DIM), F32),
                        pltpu.VMEM((hm, 1, MLSTM_QK_DIM), F32),
                        pltpu.VMEM((hm, 1, LANES), F32)],
        compiler_params=_cparams("parallel", "arbitrary"),
        name="mlstm_prompt",
    )(qm, km, vm, gates, cum_l, gates_t, cum_lt)


def _paged_kernel(pt_ref, q_ref, kn_ref, vn_ref, lfn_ref, *refs, pps):
    k_refs = refs[:pps]
    v_refs = refs[pps:2 * pps]
    lf_refs = refs[2 * pps:3 * pps]
    o_ref, lf_sc, bias_sc, m_sc, l_sc, acc_sc, carry_sc = refs[3 * pps:]
    g = pl.program_id(1)
    nh, dh = N_ATT_HEADS, ATT_HEAD_DIM
    page = k_refs[0].shape[0]
    width = page * nh
    q8 = q_ref[0]

    @pl.when(g == 0)
    def _():
        m_sc[...] = jnp.sum(q8 * kn_ref[0], axis=-1, keepdims=True)
        l_sc[...] = jnp.ones(l_sc.shape, F32)
        acc_sc[...] = vn_ref[0]
        carry_sc[...] = jnp.zeros(carry_sc.shape, F32)

    for u in range(pps):
        lf_sc[u:u + 1, :] = lf_refs[u][...]
    lf = lf_sc[...]
    lane = lax.broadcasted_iota(I32, (pps, width), 1)
    incl = lf
    k = nh
    while k < width:
        incl = incl + jnp.where(lane < width - k, pltpu.roll(incl, width - k, axis=1), 0.0)
        k *= 2
    tot = jnp.where(lane < nh, incl, 0.0)
    k = nh
    while k < width:
        tot = tot + pltpu.roll(tot, k, axis=1)
        k *= 2
    run = carry_sc[...]
    for u in range(pps):
        bias_sc[u:u + 1, :] = run
        run = run + tot[u:u + 1, :]
    carry_sc[...] = run
    bias = bias_sc[...] + (incl - lf) + lfn_ref[0]

    own = lax.broadcasted_iota(I32, (nh, width), 1) % nh == lax.broadcasted_iota(I32, (nh, width), 0)
    q_bf = q8.astype(BF16)
    scores = []
    m_old = m_sc[...]
    m_new = m_old
    for u in range(pps):
        kf = k_refs[u][...].reshape(width, dh).astype(BF16)
        s = jnp.where(own, _dot_nt(q_bf, kf) + bias[u:u + 1, :], NEG)
        m_new = jnp.maximum(m_new, jnp.max(s, axis=-1, keepdims=True))
        scores.append(s)
    alpha = jnp.exp(m_old - m_new)
    l_new = alpha * l_sc[...]
    acc = alpha * acc_sc[...]
    for u in range(pps):
        p = jnp.exp(scores[u] - m_new)
        l_new = l_new + jnp.sum(p, axis=-1, keepdims=True)
        acc = acc + _dot(p.astype(BF16), v_refs[u][...].reshape(width, dh).astype(BF16))
    m_sc[...] = m_new
    l_sc[...] = l_new
    acc_sc[...] = acc

    @pl.when(g == pl.num_programs(1) - 1)
    def _():
        o_ref[0] = acc / l_new


def _paged_attention(q, k_new, v_new, lf_new_flat, cache_k, cache_v, cache_lf_flat, page_table):
    db, n_pages = page_table.shape
    _, n_pool, page, nh, dh = cache_k.shape
    width = page * nh
    pps = PAGES_PER_STEP
    steps = n_pages // pps
    tok = lambda: pl.BlockSpec((1, nh, dh), lambda b, g, pt: (b, 0, 0))

    def page_id(b, g, pt, u):
        return pt[b * n_pages + (n_pages - 1 - (g * pps + u))]

    kv_specs = [pl.BlockSpec((None, None, page, nh, dh),
                             functools.partial(lambda b, g, pt, u: (0, page_id(b, g, pt, u), 0, 0, 0), u=u))
                for u in range(pps)]
    lf_specs = [pl.BlockSpec((None, 1, width),
                             functools.partial(lambda b, g, pt, u: (page_id(b, g, pt, u), 0, 0), u=u))
                for u in range(pps)]
    grid_spec = pltpu.PrefetchScalarGridSpec(
        num_scalar_prefetch=1,
        grid=(db, steps),
        in_specs=[tok(), tok(), tok(), pl.BlockSpec((1, 1, width), lambda b, g, pt: (b, 0, 0))]
        + kv_specs + kv_specs + lf_specs,
        out_specs=tok(),
        scratch_shapes=[pltpu.VMEM((pps, width), F32), pltpu.VMEM((pps, width), F32),
                        pltpu.VMEM((nh, 1), F32), pltpu.VMEM((nh, 1), F32),
                        pltpu.VMEM((nh, dh), F32), pltpu.VMEM((1, width), F32)],
    )
    return pl.pallas_call(
        functools.partial(_paged_kernel, pps=pps),
        grid_spec=grid_spec,
        out_shape=jax.ShapeDtypeStruct((db, nh, dh), F32),
        compiler_params=_cparams("parallel", "arbitrary"),
        name="paged_fox",
    )(page_table.reshape(-1), q, k_new, v_new, lf_new_flat, *([cache_k] * pps), *([cache_v] * pps),
      *([cache_lf_flat] * pps))


def _mstep_kernel(q_ref, k_ref, v_ref, ig_ref, lf_ref, c_ref, n_ref, m_ref,
                  h_ref, c_out_ref, n_out_ref, m_out_ref):
    for bb in range(q_ref.shape[0]):
        for hh in range(N_MLSTM_HEADS):
            q = q_ref[bb, hh].astype(F32)
            k = k_ref[bb, hh].astype(F32)
            v = v_ref[bb, hh].astype(F32)
            ig = ig_ref[bb, hh]
            lf = lf_ref[bb, hh]
            c0 = c_ref[bb, hh]
            n0 = n_ref[bb, hh]
            m0 = m_ref[bb, hh]
            inter = lf + m0
            m = jnp.maximum(inter, ig)
            w_intra = jnp.exp(ig - m)
            w_inter = jnp.exp(inter - m)
            s = jnp.sum(q * k, axis=0, keepdims=True) * w_intra
            num = s * v + w_inter * jnp.sum(q * c0, axis=0, keepdims=True)
            den = s + w_inter * jnp.sum(q * n0, axis=0, keepdims=True)
            h_ref[bb, hh] = num / jnp.maximum(jnp.abs(den), jnp.exp(-m))
            c_out_ref[bb, hh] = w_inter * c0 + w_intra * (k * v)
            n_out_ref[bb, hh] = w_inter * n0 + w_intra * k
            m_out_ref[bb, hh] = m


MSTEP_ROWS = 4


def _mlstm_step(q_col, k_col, v_row, ig, lf, state_c, state_n_col, state_m):
    db, hm, dk, dv = state_c.shape
    tb = MSTEP_ROWS if db % MSTEP_ROWS == 0 else 1
    spec = lambda a, b_: pl.BlockSpec((tb, hm, a, b_), lambda i: (i, 0, 0, 0))
    return pl.pallas_call(
        _mstep_kernel,
        grid=(db // tb,),
        in_specs=[spec(dk, 1), spec(dk, 1), spec(1, dv), spec(1, 1), spec(1, 1), spec(dk, dv), spec(dk, 1), spec(1, 1)],
        out_specs=(spec(1, dv), spec(dk, dv), spec(dk, 1), spec(1, 1)),
        out_shape=(jax.ShapeDtypeStruct((db, hm, 1, dv), F32), jax.ShapeDtypeStruct((db, hm, dk, dv), F32),
                   jax.ShapeDtypeStruct((db, hm, dk, 1), F32), jax.ShapeDtypeStruct((db, hm, 1, 1), F32)),
        compiler_params=_cparams("parallel"),
        name="mlstm_step",
    )(q_col, k_col, v_row, ig, lf, state_c, state_n_col, state_m)


def _mix_kernel(att_ref, hm_ref, og_ref, h_ref, gm_ref, w_ref, lg_ref, lb_ref, wrh_ref, wrl_ref, br_ref,
                h1_ref, h1p_ref, e_ref, gate_ref, *, alpha):
    d_att = att_ref.shape[-1]
    parts = []
    for hh in range(N_MLSTM_HEADS):
        vv = slice(hh * MLSTM_V_DIM, (hh + 1) * MLSTM_V_DIM)
        x = hm_ref[:, vv]
        xn = x * lax.rsqrt(jnp.mean(x * x, axis=-1, keepdims=True) + RMS_EPS) * gm_ref[:, vv]
        parts.append((og_ref[:, vv] * xn).astype(BF16))
    mix = _dot(att_ref[...].astype(BF16), w_ref[0:d_att, :])
    for hh in range(N_MLSTM_HEADS):
        r0 = d_att + hh * MLSTM_V_DIM
        mix = mix + _dot(parts[hh], w_ref[r0:r0 + MLSTM_V_DIM, :])
    h1 = _layer_norm(alpha * h_ref[...] + mix, lg_ref[...], lb_ref[...])
    h1_ref[...] = h1
    half = h1.shape[-1] // 2
    hi = lax.bitcast_convert_type(h1[:, :half].astype(BF16).astype(F32), jnp.uint32)
    lo = lax.bitcast_convert_type(h1[:, half:].astype(BF16).astype(F32), jnp.uint32)
    h1p_ref[...] = hi | (lo >> 16)

    logits = _dot3(h1, wrh_ref[...], wrl_ref[...]) + br_ref[...]
    lane = lax.broadcasted_iota(I32, logits.shape, 1).astype(F32)
    e_out = jnp.zeros(logits.shape, F32)
    top = []
    for kk in range(TOP_K):
        mx = jnp.max(logits, axis=-1, keepdims=True)
        idx = jnp.min(jnp.where(logits == mx, lane, float(LANES)), axis=-1, keepdims=True)
        logits = jnp.where(lane == idx, -jnp.inf, logits)
        e_out = jnp.where(lane == float(kk), idx, e_out)
        top.append(mx)
    ex = [jnp.exp(t - top[0]) for t in top]
    inv = 1.0 / (ex[0] + ex[1] + ex[2] + ex[3])
    gate = jnp.zeros(logits.shape, F32)
    for kk in range(TOP_K):
        gate = jnp.where(lane == float(kk), ex[kk] * inv, gate)
    e_ref[...] = e_out.astype(I32)
    gate_ref[...] = gate


def _mix(att, hm, og, h, prm, tm, alpha, packed_rows=None):
    rows, d = h.shape
    packed_rows = rows if packed_rows is None else packed_rows
    d_att = att.shape[-1]
    d_mv = hm.shape[-1]
    row = lambda w: pl.BlockSpec((tm, w), lambda i: (i, 0))
    const = lambda r, c: pl.BlockSpec((r, c), lambda i: (0, 0), pipeline_mode=pl.Buffered(1))
    return pl.pallas_call(
        functools.partial(_mix_kernel, alpha=alpha),
        grid=(rows // tm,),
        in_specs=[row(d_att), row(d_mv), row(d_mv), row(d), const(1, d_mv), const(d_att + d_mv, d),
                  const(1, d), const(1, d), const(d, LANES), const(d, LANES), const(1, LANES)],
        out_specs=(row(d), row(d // 2), row(LANES), row(LANES)),
        out_shape=(jax.ShapeDtypeStruct((rows, d), F32), jax.ShapeDtypeStruct((packed_rows, d // 2), jnp.uint32),
                   jax.ShapeDtypeStruct((rows, LANES), I32), jax.ShapeDtypeStruct((rows, LANES), F32)),
        compiler_params=_cparams("parallel"),
        name="mix",
    )(att, hm, og, h, prm["g_m_out"], prm["w_out"], prm["ln1_g"], prm["ln1_b"], prm["wr_hi"], prm["wr_lo"],
      prm["b_router"])


def _moe_kernel(ie_ref, ns_ref, iv_ref, st_ref, sp_ref,
                x_hbm, wg_ref, wu_ref, bg_ref, bu_ref, wd_ref, bd_ref,
                y_hbm, xbuf, xb, acc, gsem, ssem, *, n_tok, n_items, n_ft):
    i = pl.program_id(0)
    f = pl.program_id(1)
    nf = pl.num_programs(1)
    slot = i % 2
    other = 1 - slot
    n_sub = ns_ref[i]
    valid = iv_ref[i] == 1
    half = xbuf.shape[-1]
    dump_base = TOP_K * n_tok

    def gather_rows(item, sl, row0, count):
        base = st_ref[item] + row0
        dst = xbuf.at[sl, pl.ds(row0, count), :]
        for r in range(count):
            pair = sp_ref[base + r]
            tok = jnp.maximum(pair, 0) >> LOG2_TOP_K
            pltpu.make_async_copy(x_hbm.at[pl.ds(tok, 1), :], dst.at[pl.ds(r, 1), :], gsem.at[sl]).start()

    def scatter_rows(item, sl, row0, count):
        base = st_ref[item] + row0
        n_real = ns_ref[item] * MOE_SUB - row0
        dump = dump_base + sl * MOE_ROWS + row0
        src = acc.at[sl, pl.ds(row0, count), :]
        for r in range(count):
            pair = sp_ref[base + r]
            real = jnp.logical_and(pair >= 0, r < n_real)
            dst = jnp.where(real, (pair & (TOP_K - 1)) * n_tok + (pair >> LOG2_TOP_K), dump + r)
            pltpu.make_async_copy(src.at[pl.ds(r, 1), :], y_hbm.at[pl.ds(dst, 1), :], ssem.at[sl]).start()

    def gather_wait(sl):
        pltpu.make_async_copy(x_hbm.at[pl.ds(0, MOE_ROWS), :], xbuf.at[sl], gsem.at[sl]).wait()

    def scatter_wait(sl):
        pltpu.make_async_copy(acc.at[sl], y_hbm.at[pl.ds(0, MOE_ROWS), :], ssem.at[sl]).wait()

    def all_steps(fn):
        def body(step, carry):
            fn(pl.multiple_of(step * MOE_SUB, MOE_SUB))
            return carry
        lax.fori_loop(0, MOE_ROWS // MOE_SUB, body, 0)

    nxt = jnp.minimum(i + 1, n_items - 1)
    has_next = jnp.logical_and(i + 1 < n_items, iv_ref[nxt] == 1)
    prv = jnp.maximum(i - 1, 0)
    has_prev = jnp.logical_and(i >= 1, iv_ref[prv] == 1)
    prv2 = jnp.maximum(i - 2, 0)
    has_prev2 = jnp.logical_and(i >= 2, iv_ref[prv2] == 1)

    @pl.when(f == 0)
    def _():
        @pl.when(jnp.logical_and(i == 0, valid))
        def _():
            all_steps(lambda row0: gather_rows(0, 0, row0, MOE_SUB))

        @pl.when(valid)
        def _():
            gather_wait(slot)
            packed = xbuf[slot]
            hi = lax.bitcast_convert_type(packed & jnp.uint32(0xFFFF0000), F32)
            lo = lax.bitcast_convert_type(packed << 16, F32)
            xb[:, 0:half] = hi.astype(BF16)
            xb[:, half:2 * half] = lo.astype(BF16)

        @pl.when(has_prev2)
        def _():
            scatter_wait(slot)

        @pl.when(valid)
        def _():
            acc[slot] = jnp.broadcast_to(bd_ref[0], acc.shape[1:])

    per_step = MOE_ROWS // n_ft
    row0 = pl.multiple_of(f * per_step, per_step)
    per_block = MOE_BLOCK // MOE_SUB
    n_full = n_sub // per_block
    max_full = MOE_ROWS // MOE_BLOCK
    share = per_step // max_full
    fused = jnp.logical_and(jnp.logical_and(valid, n_full == max_full), jnp.logical_and(has_next, has_prev))

    @pl.when(jnp.logical_and(has_next, jnp.logical_not(fused)))
    def _():
        gather_rows(i + 1, other, row0, per_step)

    @pl.when(jnp.logical_and(has_prev, jnp.logical_not(fused)))
    def _():
        scatter_rows(i - 1, other, row0, per_step)

    @pl.when(valid)
    def _():
        wg = wg_ref[0].astype(BF16)
        wu = wu_ref[0].astype(BF16)
        wd = wd_ref[0].astype(BF16)

        def rows_block(start, count):
            rows = pl.ds(start, count)
            xs = xb[rows, :]
            gp = _dot(xs, wg) + bg_ref[0]
            up = _dot(xs, wu) + bu_ref[0]
            gp = jnp.minimum(gp, SWIGLU_LIMIT)
            up = jnp.clip(up, -SWIGLU_LIMIT, SWIGLU_LIMIT)
            act = gp * _sigmoid(SWIGLU_ALPHA * gp) * (up + 1.0)
            acc[slot, rows, :] += _dot(act.astype(BF16), wd)

        def body(blk, carry):
            rows_block(pl.multiple_of(blk * MOE_BLOCK, MOE_BLOCK), MOE_BLOCK)
            return carry

        def body_fused(blk, carry):
            r0 = pl.multiple_of(row0 + blk * share, 8)
            gather_rows(i + 1, other, r0, share)
            scatter_rows(i - 1, other, r0, share)
            rows_block(pl.multiple_of(blk * MOE_BLOCK, MOE_BLOCK), MOE_BLOCK)
            return carry

        @pl.when(fused)
        def _():
            lax.fori_loop(0, max_full, body_fused, 0)

        @pl.when(jnp.logical_not(fused))
        def _():
            lax.fori_loop(0, n_full, body, 0)

        tail = n_sub - n_full * per_block
        for nn in range(1, per_block):
            @pl.when(tail == nn)
            def _():
                rows_block(pl.multiple_of(n_full * MOE_BLOCK, MOE_BLOCK), nn * MOE_SUB)

    @pl.when(jnp.logical_and(i == n_items - 1, f == nf - 1))
    def _():
        @pl.when(valid)
        def _():
            all_steps(lambda row0: scatter_rows(i, slot, row0, MOE_SUB))
            scatter_wait(slot)

        @pl.when(has_prev)
        def _():
            scatter_wait(other)


def _moe(x_packed, item_e, item_ns, item_valid, item_start, slot_pair, w_gate_up, b_gate_up, w_down, b_down):
    n_tok, half = x_packed.shape
    d = 2 * half
    n_exp, _, two_f = w_gate_up.shape
    nf = two_f // 2 // MOE_FT
    n_items = item_e.shape[0]
    assert MOE_ROWS % nf == 0 and (MOE_ROWS // nf) % (8 * (MOE_ROWS // MOE_BLOCK)) == 0

    def ftile(i, f, iv):
        return f * iv[i] + (nf - 1) * (1 - iv[i])

    grid_spec = pltpu.PrefetchScalarGridSpec(
        num_scalar_prefetch=5,
        grid=(n_items, nf),
        in_specs=[
            pl.BlockSpec(memory_space=pl.ANY),
            pl.BlockSpec((1, d, MOE_FT), lambda i, f, ie, ns, iv, st, sp: (ie[i], 0, ftile(i, f, iv))),
            pl.BlockSpec((1, d, MOE_FT), lambda i, f, ie, ns, iv, st, sp: (ie[i], 0, nf + ftile(i, f, iv))),
            pl.BlockSpec((1, 1, MOE_FT), lambda i, f, ie, ns, iv, st, sp: (ie[i], 0, ftile(i, f, iv))),
            pl.BlockSpec((1, 1, MOE_FT), lambda i, f, ie, ns, iv, st, sp: (ie[i], 0, nf + ftile(i, f, iv))),
            pl.BlockSpec((1, MOE_FT, d), lambda i, f, ie, ns, iv, st, sp: (ie[i], ftile(i, f, iv), 0)),
            pl.BlockSpec((1, 1, d), lambda i, f, ie, ns, iv, st, sp: (ie[i], 0, 0)),
        ],
        out_specs=pl.BlockSpec(memory_space=pl.ANY),
        scratch_shapes=[pltpu.VMEM((2, MOE_ROWS, half), jnp.uint32), pltpu.VMEM((MOE_ROWS, d), BF16),
                        pltpu.VMEM((2, MOE_ROWS, d), F32),
                        pltpu.SemaphoreType.DMA((2,)), pltpu.SemaphoreType.DMA((2,))],
    )
    return pl.pallas_call(
        functools.partial(_moe_kernel, n_tok=n_tok, n_items=n_items, n_ft=nf),
        grid_spec=grid_spec,
        out_shape=jax.ShapeDtypeStruct((TOP_K * n_tok + 2 * MOE_ROWS, d), F32),
        compiler_params=_cparams("arbitrary", "arbitrary"),
        name="moe",
    )(item_e, item_ns, item_valid, item_start, slot_pair, x_packed, w_gate_up, w_gate_up,
      b_gate_up.reshape(n_exp, 1, two_f), b_gate_up.reshape(n_exp, 1, two_f), w_down, b_down.reshape(n_exp, 1, d))


def _combine_kernel(y0_ref, y1_ref, y2_ref, y3_ref, gate_ref, h1_ref, lg_ref, lb_ref, o_ref, *, alpha):
    acc = alpha * h1_ref[...]
    for kk, y_ref in enumerate((y0_ref, y1_ref, y2_ref, y3_ref)):
        acc = acc + y_ref[...] * gate_ref[:, kk:kk + 1]
    o_ref[0] = _layer_norm(acc, lg_ref[...], lb_ref[...])


def _combine(y_pairs, gate, h1, ln_g, ln_b, alpha, *, n_tok, tok0, group_rows, skip, out_rows, tm):
    d = h1.shape[-1]
    groups = h1.shape[0] // group_rows
    assert out_rows % tm == 0 and skip % 8 == 0 and group_rows % 8 == 0 and tok0 % 8 == 0 and n_tok % 8 == 0

    def rows(base):
        return lambda b, j: (pl.multiple_of(base + b * group_rows + skip + j * tm, 8), 0)

    def el(w, base):
        return pl.BlockSpec((pl.Element(tm), pl.Element(w)), rows(base))

    const = lambda c: pl.BlockSpec((1, c), lambda b, j: (0, 0))
    y_specs = [el(d, kk * n_tok + tok0) for kk in range(TOP_K)]
    return pl.pallas_call(
        functools.partial(_combine_kernel, alpha=alpha),
        grid=(groups, out_rows // tm),
        in_specs=y_specs + [el(LANES, 0), el(d, 0), const(d), const(d)],
        out_specs=pl.BlockSpec((1, tm, d), lambda b, j: (b, j, 0)),
        out_shape=jax.ShapeDtypeStruct((groups, out_rows, d), F32),
        compiler_params=_cparams("parallel", "parallel"),
        name="combine",
    )(*([y_pairs] * TOP_K), gate, h1, ln_g, ln_b)


def _route(top_e):
    n_pairs = top_e.shape[0] * TOP_K
    per_item = MOE_ROWS // MOE_SUB
    pair_e = top_e.reshape(-1)
    order = jnp.argsort(pair_e, stable=True).astype(I32)
    e_sorted = pair_e[order]
    edges = jnp.arange(N_EXPERTS + 1, dtype=I32)
    bounds = jnp.sum((e_sorted[None, :] < edges[:, None]).astype(I32), axis=1)
    start = bounds[:-1]
    counts = bounds[1:] - start
    n_sb = (counts + MOE_SUB - 1) // MOE_SUB
    sb_end = jnp.cumsum(n_sb)
    sb_start = sb_end - n_sb
    max_sb = n_pairs // MOE_SUB + N_EXPERTS
    slot = jnp.arange(max_sb * MOE_SUB, dtype=I32)
    e_slot = jnp.minimum(jnp.sum((slot[:, None] >= (sb_end * MOE_SUB)[None, :]).astype(I32), axis=1), N_EXPERTS - 1)
    rank = slot - sb_start[e_slot] * MOE_SUB
    src = jnp.clip(start[e_slot] + rank, 0, n_pairs - 1)
    slot_pair = jnp.where(rank < counts[e_slot], order[src], -1).astype(I32)
    slot_pair = jnp.concatenate([slot_pair, jnp.full((MOE_ROWS,), -1, I32)])

    n_it = (n_sb + per_item - 1) // per_item
    it_end = jnp.cumsum(n_it)
    it_start = it_end - n_it
    max_items = (max_sb + (per_item - 1) * N_EXPERTS) // per_item
    idx = jnp.arange(max_items, dtype=I32)
    n_valid = it_end[-1]
    item_valid = (idx < n_valid).astype(I32)
    idc = jnp.minimum(idx, n_valid - 1)
    item_e = jnp.minimum(jnp.sum((idc[:, None] >= it_end[None, :]).astype(I32), axis=1), N_EXPERTS - 1).astype(I32)
    chunk = idc - it_start[item_e]
    item_start = ((sb_start[item_e] + per_item * chunk) * MOE_SUB).astype(I32)
    item_ns = (jnp.clip(n_sb[item_e] - per_item * chunk, 0, per_item) * item_valid).astype(I32)
    return item_e, item_ns, item_valid, item_start, slot_pair


def _prep_params(ln_in_g, ln_in_b, w_in, b_att_f, g_att_q, g_att_k, b_m_i, b_m_f, g_m_out, w_out,
                 ln1_g, ln1_b, w_router, b_router, ln2_g, ln2_b):
    d = w_in.shape[0]
    d_att = N_ATT_HEADS * ATT_HEAD_DIM
    d_mqk = N_MLSTM_HEADS * MLSTM_QK_DIM
    d_mv = N_MLSTM_HEADS * MLSTM_V_DIM
    sizes = (d_att, d_att, d_att, N_ATT_HEADS, d_mqk, d_mqk, d_mv, N_MLSTM_HEADS, N_MLSTM_HEADS, d_mv)
    offs = [0]
    for s in sizes:
        offs.append(offs[-1] + s)
    cols = lambda i: w_in[:, offs[i]:offs[i + 1]]
    w_main = jnp.concatenate([cols(0), cols(1), cols(2), cols(4), cols(5), cols(6), cols(9)], axis=1).astype(BF16)
    w_gate = jnp.concatenate([cols(3), cols(7), cols(8), jnp.zeros((d, LANES - N_GATES), F32)], axis=1)
    wg_hi, wg_lo = _split2(w_gate)
    gate_bias = jnp.concatenate([b_att_f, b_m_i, b_m_f, jnp.zeros((LANES - N_GATES,), F32)]).reshape(1, LANES)
    w_r = jnp.concatenate([w_router, jnp.zeros((d, LANES - N_EXPERTS), F32)], axis=1)
    wr_hi, wr_lo = _split2(w_r)
    b_r = jnp.concatenate([b_router, jnp.full((LANES - N_EXPERTS,), NEG, F32)]).reshape(1, LANES)
    return dict(
        ln_in_g=ln_in_g.reshape(1, d), ln_in_b=ln_in_b.reshape(1, d), w_main=w_main, wg_hi=wg_hi, wg_lo=wg_lo,
        gate_bias=gate_bias, g_att_q=g_att_q.reshape(1, -1), g_att_k=g_att_k.reshape(1, -1),
        g_m_out=g_m_out.reshape(1, -1), w_out=w_out.astype(BF16), ln1_g=ln1_g.reshape(1, d),
        ln1_b=ln1_b.reshape(1, d), wr_hi=wr_hi, wr_lo=wr_lo, b_router=b_r,
        ln2_g=ln2_g.reshape(1, d), ln2_b=ln2_b.reshape(1, d))


def kernel(x_prompt, x_sample, cache_k, cache_v, cache_logf, state_C, state_n, state_m, page_table,
           meta_tokens, ln_in_g, ln_in_b, w_in, b_att_f, g_att_q, g_att_k, b_m_i, b_m_f, g_m_out, w_out,
           ln1_g, ln1_b, w_router, b_router, w_gate_up, b_gate_up, w_down, b_down, ln2_g, ln2_b):
    depth = w_in.shape[0]
    assert depth == 1, "single-layer trunk only"
    nb, seq, d = x_prompt.shape
    db = x_sample.shape[0]
    assert x_sample.shape[1] == 1, "one new token per sample"
    alpha = (2.0 * depth) ** 0.25
    d_att = N_ATT_HEADS * ATT_HEAD_DIM
    hm = N_MLSTM_HEADS
    prm = _prep_params(ln_in_g, ln_in_b, w_in[0], b_att_f[0], g_att_q[0], g_att_k[0], b_m_i[0], b_m_f[0],
                       g_m_out[0], w_out[0], ln1_g[0], ln1_b[0], w_router[0], b_router[0], ln2_g[0], ln2_b[0])

    length = N_META + seq
    lp = pl.cdiv(length, CHUNK) * CHUNK
    meta = jnp.broadcast_to(meta_tokens[None], (nb, N_META, d))
    xp = jnp.pad(x_prompt, ((0, 0), (N_META, lp - length), (0, 0)))
    xp = lax.dynamic_update_slice(xp, meta, (0, 0, 0))
    (hp, qa, k32, kb, v32, vb, gates, qm, km, vm, og) = _proj(xp, length, lp // 8, prm)
    cum_l, cum_lt, gates_t, k_aug, q_aug = _gate_scan(gates)
    att_p = _fox_prompt(qa, q_aug, kb, k_aug, vb)
    hm_p, c_p, n_p, m_p = _mlstm_prompt(qm, km, vm, gates, cum_l, gates_t, cum_lt)
    rows_p = nb * lp
    h1_p, h1p_p, e_p, gate_p = _mix(att_p.reshape(rows_p, d_att), hm_p.reshape(rows_p, -1), og.reshape(rows_p, -1),
                                    hp.reshape(rows_p, d), prm, 256, alpha, packed_rows=rows_p + db)

    xs = x_sample.reshape(1, db, d)
    (hs, qa_s, k32_s, _, v32_s, _, gates_s, qm_s, km_s, vm_s, og_s) = _proj(xs, db, db, prm)
    n_pool, page = cache_k.shape[1], cache_k.shape[2]
    nh, dh = N_ATT_HEADS, ATT_HEAD_DIM
    lf_s = gates_s[0, :, GATE_LF_ATT:GATE_LF_ATT + nh]
    att_s = _paged_attention(
        qa_s.astype(F32).reshape(db, nh, dh), k32_s.reshape(db, nh, dh), v32_s.reshape(db, nh, dh),
        jnp.tile(lf_s, (1, page)).reshape(db, 1, page * nh), cache_k, cache_v,
        cache_logf[0].reshape(n_pool, 1, page * nh), page_table)
    ig_s = gates_s[0, :, GATE_IG:GATE_IG + hm].reshape(db, hm, 1, 1)
    lfm_s = gates_s[0, :, GATE_LF_M:GATE_LF_M + hm].reshape(db, hm, 1, 1)
    hm_s, c_s, n_s, m_s = _mlstm_step(
        qm_s.astype(F32).reshape(db, hm, MLSTM_QK_DIM, 1), km_s.astype(F32).reshape(db, hm, MLSTM_QK_DIM, 1),
        vm_s.astype(F32).reshape(db, hm, 1, MLSTM_V_DIM), ig_s, lfm_s, state_C[0],
        state_n[0].reshape(db, hm, MLSTM_QK_DIM, 1), state_m[0].reshape(db, hm, 1, 1))
    h1_s, h1p_s, e_s, gate_s = _mix(att_s.reshape(db, d_att), hm_s.reshape(db, -1), og_s.reshape(db, -1),
                                    hs.reshape(db, d), prm, db, alpha)

    n_tok = rows_p + db
    x_packed = lax.dynamic_update_slice(h1p_p, h1p_s, (rows_p, 0))
    top_e = jnp.concatenate([e_p, e_s], axis=0)[:, :TOP_K]
    item_e, item_ns, item_valid, item_start, slot_pair = _route(top_e)
    y_pairs = _moe(x_packed, item_e, item_ns, item_valid, item_start, slot_pair,
                   w_gate_up[0], b_gate_up[0], w_down[0], b_down[0])
    y_prompt = _combine(y_pairs, gate_p, h1_p, prm["ln2_g"], prm["ln2_b"], alpha, n_tok=n_tok, tok0=0,
                        group_rows=lp, skip=N_META, out_rows=seq, tm=256)
    y_sample = _combine(y_pairs, gate_s, h1_s, prm["ln2_g"], prm["ln2_b"], alpha, n_tok=n_tok, tok0=rows_p,
                        group_rows=db, skip=0, out_rows=db, tm=db).reshape(db, 1, d)
    k_prompt = k32.reshape(1, nb, length, N_ATT_HEADS, ATT_HEAD_DIM)
    v_prompt = v32.reshape(1, nb, length, N_ATT_HEADS, ATT_HEAD_DIM)
    lf_prompt = gates[:, :length, GATE_LF_ATT:GATE_LF_ATT + N_ATT_HEADS][None]
    return (y_prompt, y_sample, k_prompt, v_prompt, lf_prompt,
            c_p[None], n_p.reshape(1, nb, hm, MLSTM_QK_DIM), m_p[:, :, 0, 0][None],
            k32_s.reshape(1, db, 1, N_ATT_HEADS, ATT_HEAD_DIM), v32_s.reshape(1, db, 1, N_ATT_HEADS, ATT_HEAD_DIM),
            lf_s.reshape(1, db, 1, nh),
            c_s[None], n_s.reshape(1, db, hm, MLSTM_QK_DIM), m_s.reshape(1, db, hm))
```
